```python
import math
import jax, jax.numpy as jnp
from jax import lax
import numpy as np

D_MODEL = 2048
BATCH = 8
SEQ = 2048
DEPTH = 1
DEC_BATCH = 32
DEC_SEQ = 4
PAST_LEN = 8192
PAGE_SIZE = 128

HEAD_DIM = 64
N_FOX = 16
N_NSA = 16
N_KV = 4
HPG = N_NSA // N_KV
D_FOX = N_FOX * HEAD_DIM
D_NSA = N_NSA * HEAD_DIM
D_KV = N_KV * HEAD_DIM
D_MIX = D_FOX + D_NSA
D_IN = 3 * D_FOX + N_FOX + D_NSA + 6 * D_KV + 3 * N_NSA
D_FF = 5632
L_CMP = 32
STRIDE = 16
L_SEL = 64
N_SEL = 16
N_LOCAL = 2
WINDOW = 512
PHI_HIDDEN = 128
Q_BLOCK = 128
SEL_Q_BLOCK = 32
RMS_EPS = 1e-6
NEG = -1e30
FORGET_BIAS = 2.0

kernel_name = 'hymba_fox_nsa_macaron_step'


def rmsnorm(x, g):
    xf = x.astype(jnp.float32)
    y = xf * lax.rsqrt(jnp.mean(xf * xf, axis=-1, keepdims=True) + RMS_EPS)
    return (y * g.astype(jnp.float32)).astype(x.dtype)


def swiglu(x, w_gate, w_up, w_down):
    return (jax.nn.silu(x @ w_gate) * (x @ w_up)) @ w_down


def masked_softmax(s, mask):
    s = jnp.where(mask, s.astype(jnp.float32), NEG)
    e = jnp.exp(s - jnp.max(s, axis=-1, keepdims=True)) * mask
    return e / jnp.maximum(jnp.sum(e, axis=-1, keepdims=True), 1e-30)


def alibi_slopes(n):
    return 2.0 ** (-8.0 * jnp.arange(1, n + 1, dtype=jnp.float32) / n)


def project(h, w_in, b_forget, b_gate):
    B, T, _ = h.shape
    sizes = (D_FOX, D_FOX, D_FOX, N_FOX, D_NSA) + (D_KV,) * 6 + (3 * N_NSA,)
    parts = jnp.split(h @ w_in, np.cumsum(sizes)[:-1].tolist(), axis=-1)
    qf, kf, vf = (p.reshape(B, T, N_FOX, HEAD_DIM) for p in parts[0:3])
    log_f = jax.nn.log_sigmoid((parts[3] + b_forget).astype(jnp.float32))
    qn = parts[4].reshape(B, T, N_KV, HPG, HEAD_DIM)
    kc, vc, ks, vs, kw, vw = (p.reshape(B, T, N_KV, HEAD_DIM) for p in parts[5:11])
    gates = jax.nn.sigmoid(parts[11] + b_gate).reshape(B, T, 3, N_KV, HPG)
    return qf, kf, vf, log_f, qn, kc, vc, ks, vs, kw, vw, gates


def fox_attend(q, c_q, k, v, c_k):
    B, Tq, H, d = q.shape
    Tk = k.shape[1]
    qb = math.gcd(Q_BLOCK, Tq)
    nb = Tq // qb
    q0 = Tk - Tq
    k_pos = jnp.arange(Tk)
    ck = jnp.swapaxes(c_k, 1, 2)[:, :, None, :]
    scale = HEAD_DIM ** -0.5

    def block(args):
        qi, cqi, i = args
        q_pos = q0 + i * qb + jnp.arange(qb)
        s = jnp.einsum('bqhd,bkhd->bhqk', qi, k).astype(jnp.float32) * scale
        s = s + (jnp.swapaxes(cqi, 1, 2)[..., None] - ck)
        p = masked_softmax(s, (k_pos[None, :] <= q_pos[:, None])[None, None])
        return jnp.einsum('bhqk,bkhd->bqhd', p.astype(v.dtype), v)

    qs = jnp.swapaxes(q.reshape(B, nb, qb, H, d), 0, 1)
    cs = jnp.swapaxes(c_q.reshape(B, nb, qb, H), 0, 1)
    o = lax.map(block, (qs, cs, jnp.arange(nb)))
    return jnp.swapaxes(o, 0, 1).reshape(B, Tq, H * d)


def compress(k, pe, w1, w2):
    B, Tk, G, d = k.shape
    nc = (Tk - L_CMP) // STRIDE + 1
    idx = jnp.arange(nc)[:, None] * STRIDE + jnp.arange(L_CMP)[None, :]
    blk = k[:, idx] + pe[:, None, :]
    blk = jnp.transpose(blk, (0, 1, 3, 2, 4)).reshape(B, nc, G, L_CMP * d)
    return jax.nn.gelu(blk @ w1) @ w2


def nsa_block(q, gates, q_pos, k_cmp, v_cmp, ks_b, vs_b, kw, vw, kw_pos, kw_start, slopes):
    B, qb = q.shape[:2]
    f32 = jnp.float32
    scale = HEAD_DIM ** -0.5
    qpf = q_pos.astype(f32)
    sl = slopes[None, None, :, :, None]
    nc = k_cmp.shape[1]
    c_end = jnp.arange(nc) * STRIDE + (L_CMP - 1)
    s_c = jnp.einsum('bqghd,bngd->bqghn', q, k_cmp).astype(f32) * scale
    s_c = s_c - sl * (qpf[:, None] - c_end.astype(f32)[None, :])[None, :, None, None, :]
    p_c = masked_softmax(s_c, (c_end[None, :] <= q_pos[:, None])[None, :, None, None, :])
    o_c = jnp.einsum('bqghn,bngd->bqghd', p_c.astype(v_cmp.dtype), v_cmp)
    nsb = ks_b.shape[2]
    r, c = L_SEL // STRIDE, L_CMP // STRIDE
    offs, counts = np.unique((np.arange(r)[:, None] - np.arange(c)[None, :]).ravel(), return_counts=True)
    cidx = r * np.arange(nsb)[:, None] + offs[None, :]
    cw = np.where((cidx >= 0) & (cidx < nc), counts[None, :], 0).astype(np.float32)
    p_grp = jnp.sum(p_c, axis=3)
    p_slc = jnp.sum(p_grp[..., np.clip(cidx, 0, nc - 1)] * cw, axis=-1)
    jb = jnp.arange(nsb)
    cur = q_pos // L_SEL
    valid = jb[None, :] <= cur[:, None]
    forced = valid & ((jb[None, :] == 0) | ((cur[:, None] - jb[None, :]) < N_LOCAL))
    score = jnp.where(valid[None, :, None, :], p_slc, -jnp.inf)
    score = jnp.where(forced[None, :, None, :], jnp.inf, score)
    n_sel = min(N_SEL, nsb)
    _, sel = lax.top_k(score, n_sel)
    bi = jnp.arange(B)[:, None, None, None]
    gi = jnp.arange(N_KV)[None, None, :, None]
    k_s = ks_b[bi, gi, sel].reshape(B, qb, N_KV, n_sel * L_SEL, HEAD_DIM)
    v_s = vs_b[bi, gi, sel].reshape(B, qb, N_KV, n_sel * L_SEL, HEAD_DIM)
    tok = (sel[..., None] * L_SEL + jnp.arange(L_SEL)).reshape(B, qb, N_KV, 1, n_sel * L_SEL)
    s_s = jnp.einsum('bqghd,bqgmd->bqghm', q, k_s).astype(f32) * scale
    s_s = s_s - sl * (qpf[None, :, None, None, None] - tok.astype(f32))
    p_s = masked_softmax(s_s, tok <= q_pos[None, :, None, None, None])
    o_s = jnp.einsum('bqghm,bqgmd->bqghd', p_s.astype(v_s.dtype), v_s)
    s_w = jnp.einsum('bqghd,bkgd->bqghk', q, kw).astype(f32) * scale
    dist = q_pos[:, None] - kw_pos[None, :]
    s_w = s_w - sl * dist.astype(f32)[None, :, None, None, :]
    m_w = (dist >= 0) & (dist < WINDOW) & (kw_pos >= kw_start)[None, :]
    p_w = masked_softmax(s_w, m_w[None, :, None, None, :])
    o_w = jnp.einsum('bqghk,bkgd->bqghd', p_w.astype(vw.dtype), vw)
    return (gates[:, :, 0, :, :, None] * o_c + gates[:, :, 1, :, :, None] * o_s
            + gates[:, :, 2, :, :, None] * o_w)


def nsa_attend(q, gates, kc, vc, ks, vs, kw, vw, phi_pe, phi_w1, phi_w2):
    B, Tq = q.shape[:2]
    Tk = kc.shape[1]
    Tw = kw.shape[1]
    kw_start = Tk - Tw
    k_cmp = compress(kc, phi_pe[0], phi_w1[0], phi_w2[0])
    v_cmp = compress(vc, phi_pe[1], phi_w1[1], phi_w2[1])
    nsb = -(-Tk // L_SEL)
    pad = nsb * L_SEL - Tk

    def to_blocks(a):
        a = jnp.pad(a, ((0, 0), (0, pad), (0, 0), (0, 0))).reshape(B, nsb, L_SEL, N_KV, HEAD_DIM)
        return jnp.transpose(a, (0, 3, 1, 2, 4))

    ks_b, vs_b = to_blocks(ks), to_blocks(vs)
    kw_p = jnp.pad(kw, ((0, 0), (WINDOW, 0), (0, 0), (0, 0)))
    vw_p = jnp.pad(vw, ((0, 0), (WINDOW, 0), (0, 0), (0, 0)))
    qb = math.gcd(SEL_Q_BLOCK, Tq)
    nb = Tq // qb
    q0 = Tk - Tq
    slopes = alibi_slopes(N_NSA).reshape(HPG, N_KV).T

    def step(args):
        qi, gi, i = args
        a = q0 + i * qb
        q_pos = a + jnp.arange(qb)
        p0 = a - kw_start
        kwi = lax.dynamic_slice_in_dim(kw_p, p0, WINDOW + qb, axis=1)
        vwi = lax.dynamic_slice_in_dim(vw_p, p0, WINDOW + qb, axis=1)
        kw_pos = a - WINDOW + jnp.arange(WINDOW + qb)
        return nsa_block(qi, gi, q_pos, k_cmp, v_cmp, ks_b, vs_b, kwi, vwi, kw_pos, kw_start, slopes)

    qs = jnp.swapaxes(q.reshape(B, nb, qb, N_KV, HPG, HEAD_DIM), 0, 1)
    gs = jnp.swapaxes(gates.reshape(B, nb, qb, 3, N_KV, HPG), 0, 1)
    o = lax.map(step, (qs, gs, jnp.arange(nb)))
    return jnp.swapaxes(o, 0, 1).reshape(B, Tq, D_NSA)


def merge_heads(o_f, o_n, w_out):
    return jnp.concatenate([o_f, o_n], axis=-1) @ w_out


def setup_inputs(seed: int = 0) -> dict:
    key = jax.random.key(seed)
    k = jax.random.split(key, 24)
    f32 = jnp.float32

    def nrm(i, shape, scale):
        return jax.random.normal(k[i], shape, f32) * scale

    n_pages = PAST_LEN // PAGE_SIZE
    n_used = DEC_BATCH * n_pages
    n_pool = n_used + max(1, n_used // 4)
    wb = min(WINDOW, PAST_LEN)
    page_table = jax.random.permutation(k[6], n_pool)[:n_used].reshape(DEC_BATCH, n_pages).astype(jnp.int32)
    return {
        'x_prompt': nrm(0, (BATCH, SEQ, D_MODEL), 1.0),
        'x_sample': nrm(1, (DEC_BATCH, DEC_SEQ, D_MODEL), 1.0),
        'cache_fox_kv': nrm(2, (DEPTH, n_pool, PAGE_SIZE, 2, N_FOX, HEAD_DIM), 1.0),
        'cache_fox_logf': jax.nn.log_sigmoid(FORGET_BIAS + nrm(3, (DEPTH, n_pool, PAGE_SIZE, N_FOX), 1.0)),
        'cache_nsa_kv': nrm(4, (DEPTH, n_pool, PAGE_SIZE, 4, N_KV, HEAD_DIM), 1.0),
        'state_win_kv': nrm(5, (DEPTH, DEC_BATCH, wb, 2, N_KV, HEAD_DIM), 1.0),
        'page_table': page_table,
        'norm_ffn1': 1.0 + nrm(7, (DEPTH, D_MODEL), 0.02),
        'ffn1_gate': nrm(8, (DEPTH, D_MODEL, D_FF), D_MODEL ** -0.5),
        'ffn1_up': nrm(9, (DEPTH, D_MODEL, D_FF), D_MODEL ** -0.5),
        'ffn1_down': nrm(10, (DEPTH, D_FF, D_MODEL), D_FF ** -0.5),
        'norm_mix': 1.0 + nrm(11, (DEPTH, D_MODEL), 0.02),
        'w_in': nrm(12, (DEPTH, D_MODEL, D_IN), D_MODEL ** -0.5),
        'b_forget': FORGET_BIAS + nrm(13, (DEPTH, N_FOX), 0.5),
        'b_gate': nrm(14, (DEPTH, 3 * N_NSA), 0.1),
        'phi_pe': nrm(15, (DEPTH, 2, L_CMP, HEAD_DIM), 0.1),
        'phi_w1': nrm(16, (DEPTH, 2, L_CMP * HEAD_DIM, PHI_HIDDEN), (L_CMP * HEAD_DIM) ** -0.5),
        'phi_w2': nrm(17, (DEPTH, 2, PHI_HIDDEN, HEAD_DIM), PHI_HIDDEN ** -0.5),
        'w_out': nrm(18, (DEPTH, D_MIX, D_MODEL), D_MIX ** -0.5),
        'norm_ffn2': 1.0 + nrm(19, (DEPTH, D_MODEL), 0.02),
        'ffn2_gate': nrm(20, (DEPTH, D_MODEL, D_FF), D_MODEL ** -0.5),
        'ffn2_up': nrm(21, (DEPTH, D_MODEL, D_FF), D_MODEL ** -0.5),
        'ffn2_down': nrm(22, (DEPTH, D_FF, D_MODEL), D_FF ** -0.5),
        'norm_final': 1.0 + nrm(23, (D_MODEL,), 0.02),
    }


def reference(x_prompt, x_sample, cache_fox_kv, cache_fox_logf, cache_nsa_kv, state_win_kv, page_table,
              norm_ffn1, ffn1_gate, ffn1_up, ffn1_down, norm_mix, w_in, b_forget, b_gate,
              phi_pe, phi_w1, phi_w2, w_out, norm_ffn2, ffn2_gate, ffn2_up, ffn2_down, norm_final):
    DB = x_sample.shape[0]
    T = x_prompt.shape[1]
    P = page_table.shape[1] * cache_fox_kv.shape[2]
    WB = state_win_kv.shape[2]
    WP = min(WINDOW, T)
    xp, xs = x_prompt, x_sample
    fkv_p, flf_p, nkv_p, win_p = [], [], [], []
    fkv_s, flf_s, nkv_s, win_s = [], [], [], []
    for l in range(DEPTH):
        xp = xp + 0.5 * swiglu(rmsnorm(xp, norm_ffn1[l]), ffn1_gate[l], ffn1_up[l], ffn1_down[l])
        xs = xs + 0.5 * swiglu(rmsnorm(xs, norm_ffn1[l]), ffn1_gate[l], ffn1_up[l], ffn1_down[l])

        qf, kf, vf, lf, qn, kc, vc, ks, vs, kw, vw, gt = project(rmsnorm(xp, norm_mix[l]), w_in[l], b_forget[l], b_gate[l])
        cf = jnp.cumsum(lf, axis=1)
        o_f = fox_attend(qf, cf, kf, vf, cf)
        o_n = nsa_attend(qn, gt, kc, vc, ks, vs, kw, vw, phi_pe[l], phi_w1[l], phi_w2[l])
        xp = xp + merge_heads(o_f, o_n, w_out[l])
        fkv_p.append(jnp.stack([kf, vf], axis=2))
        flf_p.append(lf)
        nkv_p.append(jnp.stack([kc, vc, ks, vs], axis=2))
        win_p.append(jnp.stack([kw, vw], axis=2)[:, T - WP:])

        qf, kf, vf, lf, qn, kc, vc, ks, vs, kw, vw, gt = project(rmsnorm(xs, norm_mix[l]), w_in[l], b_forget[l], b_gate[l])
        k_all = jnp.concatenate([cache_fox_kv[l, page_table, :, 0].reshape(DB, P, N_FOX, HEAD_DIM), kf], axis=1)
        v_all = jnp.concatenate([cache_fox_kv[l, page_table, :, 1].reshape(DB, P, N_FOX, HEAD_DIM), vf], axis=1)
        c_all = jnp.cumsum(jnp.concatenate(
            [cache_fox_logf[l, page_table].reshape(DB, P, N_FOX).astype(jnp.float32), lf], axis=1), axis=1)
        o_f = fox_attend(qf, c_all[:, P:], k_all, v_all, c_all)
        win = state_win_kv[l]
        o_n = nsa_attend(
            qn, gt,
            jnp.concatenate([cache_nsa_kv[l, page_table, :, 0].reshape(DB, P, N_KV, HEAD_DIM), kc], axis=1),
            jnp.concatenate([cache_nsa_kv[l, page_table, :, 1].reshape(DB, P, N_KV, HEAD_DIM), vc], axis=1),
            jnp.concatenate([cache_nsa_kv[l, page_table, :, 2].reshape(DB, P, N_KV, HEAD_DIM), ks], axis=1),
            jnp.concatenate([cache_nsa_kv[l, page_table, :, 3].reshape(DB, P, N_KV, HEAD_DIM), vs], axis=1),
            jnp.concatenate([win[:, :, 0], kw], axis=1),
            jnp.concatenate([win[:, :, 1], vw], axis=1),
            phi_pe[l], phi_w1[l], phi_w2[l])
        xs = xs + merge_heads(o_f, o_n, w_out[l])
        fkv_s.append(jnp.stack([kf, vf], axis=2))
        flf_s.append(lf)
        nkv_s.append(jnp.stack([kc, vc, ks, vs], axis=2))
        win_s.append(jnp.concatenate([win, jnp.stack([kw, vw], axis=2)], axis=1)[:, -WB:])

        xp = xp + 0.5 * swiglu(rmsnorm(xp, norm_ffn2[l]), ffn2_gate[l], ffn2_up[l], ffn2_down[l])
        xs = xs + 0.5 * swiglu(rmsnorm(xs, norm_ffn2[l]), ffn2_gate[l], ffn2_up[l], ffn2_down[l])

    y_prompt = rmsnorm(xp, norm_final)
    y_sample = rmsnorm(xs, norm_final)
    new_fox_kv_p = jnp.stack(fkv_p)
    new_fox_logf_p = jnp.stack(flf_p)
    new_nsa_kv_p = jnp.stack(nkv_p)
    new_win_kv_p = jnp.stack(win_p)
    new_fox_kv_s = jnp.stack(fkv_s)
    new_fox_logf_s = jnp.stack(flf_s)
    new_nsa_kv_s = jnp.stack(nkv_s)
    new_win_kv_s = jnp.stack(win_s)
    return (y_prompt, y_sample, new_fox_kv_p, new_fox_logf_p, new_nsa_kv_p, new_win_kv_p,
            new_fox_kv_s, new_fox_logf_s, new_nsa_kv_s, new_win_kv_s)
```

```python
import functools

import numpy as np
import jax
import jax.numpy as jnp
from jax import lax
from jax.experimental import pallas as pl
from jax.experimental.pallas import tpu as pltpu

F32 = jnp.float32
BF16 = jnp.bfloat16

D_MODEL = 2048
HEAD_DIM = 64
N_FOX = 16
N_NSA = 16
N_KV = 4
HPG = N_NSA // N_KV
D_FOX = N_FOX * HEAD_DIM
D_NSA = N_NSA * HEAD_DIM
D_KV = N_KV * HEAD_DIM
D_FF = 5632
L_CMP = 32
STRIDE = 16
L_SEL = 64
N_SEL = 16
N_LOCAL = 2
WINDOW = 512
RMS_EPS = 1e-6
NEG = -1e30
SCALE = HEAD_DIM ** -0.5

LANES = 128
SUBLANES = 8
MISC_W = LANES
GATE_OFF = N_FOX
SQ = SUBLANES
CMP_ROW = STRIDE * HEAD_DIM

_C_QF = 0
_C_FKV = _C_QF + D_FOX
_C_QN = _C_FKV + 2 * D_FOX
_C_NKV = _C_QN + D_NSA
_C_WIN = _C_NKV + 4 * D_KV
_C_MISC = _C_WIN + 2 * D_KV
_C_END = _C_MISC + MISC_W

_VMEM_LIMIT = 56 * 1024 * 1024


def _cparams(sem):
    return pltpu.CompilerParams(dimension_semantics=sem, vmem_limit_bytes=_VMEM_LIMIT)


def _const_spec(shape):
    nd = len(shape)
    return pl.BlockSpec(shape, lambda *_: (0,) * nd, pipeline_mode=pl.Buffered(1))


def _dot(a, b):
    return jnp.dot(a, b, preferred_element_type=F32)


def _dot_nt(a, b):
    return lax.dot_general(a, b, (((1,), (1,)), ((), ())), preferred_element_type=F32)


def _split3(x):
    hi = x.astype(BF16)
    r1 = x - hi.astype(F32)
    mid = r1.astype(BF16)
    lo = (r1 - mid.astype(F32)).astype(BF16)
    return hi, mid, lo


def _dot3(x, m):
    hi, mid, lo = _split3(x)
    return _dot(hi, m) + _dot(mid, m) + _dot(lo, m)


def _rms(x, g):
    return x * lax.rsqrt(jnp.mean(x * x, axis=-1, keepdims=True) + RMS_EPS) * g


def _masked_softmax(s, mask):
    s = jnp.where(mask, s, NEG)
    e = jnp.where(mask, jnp.exp(s - jnp.max(s, axis=-1, keepdims=True)), 0.0)
    return e / jnp.maximum(jnp.sum(e, axis=-1, keepdims=True), 1e-30)


def _lane_pick(x, idx):
    lane = lax.broadcasted_iota(jnp.int32, x.shape, 1)
    return jnp.sum(jnp.where(lane == idx, x, 0.0), axis=1, keepdims=True)


def _row_to_col(r):
    n = r.shape[1]
    eye = lax.broadcasted_iota(jnp.int32, (n, n), 0) == lax.broadcasted_iota(jnp.int32, (n, n), 1)
    return jnp.sum(jnp.where(eye, r, 0.0), axis=1, keepdims=True)


def _ffn_kernel(*refs, has_mix, final):
    it = iter(refs)
    x_ref, g_ref, wg_ref, wu_ref, wd_ref = (next(it) for _ in range(5))
    if has_mix:
        of_ref, on_ref, wo_ref = (next(it) for _ in range(3))
    if final:
        gf_ref = next(it)
    o_ref, h_sc, acc_sc = (next(it) for _ in range(3))
    j = pl.program_id(1)

    @pl.when(j == 0)
    def _():
        x = x_ref[...]
        if has_mix:
            x = x + _dot(of_ref[...], wo_ref[:D_FOX, :]) + _dot(on_ref[...], wo_ref[D_FOX:, :])
            o_ref[...] = x
        h_sc[...] = _rms(x, g_ref[...]).astype(BF16)
        acc_sc[...] = jnp.zeros_like(acc_sc)

    h = h_sc[...]
    gate = _dot(h, wg_ref[...])
    up = _dot(h, wu_ref[...])
    a = gate * (1.0 / (1.0 + jnp.exp(-gate))) * up
    acc_sc[...] += _dot(a.astype(BF16), wd_ref[...])

    @pl.when(j == pl.num_programs(1) - 1)
    def _():
        y = (o_ref[...] if has_mix else x_ref[...]) + 0.5 * acc_sc[...]
        if final:
            y = _rms(y, gf_ref[...])
        o_ref[...] = y


def _ffn(x, g, wg, wu, wd, mix=None, gf=None, tm=512, tf=512):
    n = x.shape[0]
    tm = min(tm, n)
    grid = (n // tm, D_FF // tf)
    row = lambda i, j: (i, 0)
    in_specs = [pl.BlockSpec((tm, D_MODEL), row), _const_spec((1, D_MODEL)),
                pl.BlockSpec((D_MODEL, tf), lambda i, j: (0, j)), pl.BlockSpec((D_MODEL, tf), lambda i, j: (0, j)),
                pl.BlockSpec((tf, D_MODEL), lambda i, j: (j, 0))]
    args = [x, g, wg, wu, wd]
    if mix is not None:
        o_f, o_n, w_out = mix
        in_specs += [pl.BlockSpec((tm, D_FOX), row), pl.BlockSpec((tm, D_NSA), row),
                     _const_spec((D_FOX + D_NSA, D_MODEL))]
        args += [o_f, o_n, w_out]
    if gf is not None:
        in_specs.append(_const_spec((1, D_MODEL)))
        args.append(gf)
    return pl.pallas_call(
        functools.partial(_ffn_kernel, has_mix=mix is not None, final=gf is not None),
        out_shape=jax.ShapeDtypeStruct((n, D_MODEL), F32),
        grid=grid, in_specs=in_specs, out_specs=pl.BlockSpec((tm, D_MODEL), row),
        scratch_shapes=[pltpu.VMEM((tm, D_MODEL), BF16), pltpu.VMEM((tm, D_MODEL), F32)],
        compiler_params=_cparams(("parallel", "arbitrary")), name="ffn")(*args)


def _proj_kernel(x_ref, g_ref, w_ref, b_ref, qf_ref, fkv_ref, qn_ref, nkv_ref, kcvc_ref, win_ref, misc_ref):
    h = _rms(x_ref[...], g_ref[...]).astype(BF16)

    def mm(lo, hi):
        return _dot(h, w_ref[:, lo:hi])

    qf_ref[...] = (mm(_C_QF, _C_FKV) * SCALE).astype(BF16)
    fkv_ref[...] = mm(_C_FKV, _C_QN)
    qn_ref[...] = (mm(_C_QN, _C_NKV) * SCALE).astype(BF16)
    nkv = mm(_C_NKV, _C_WIN)
    nkv_ref[...] = nkv
    for j in range(2 * N_KV):
        kcvc_ref[j] = nkv[:, j * HEAD_DIM:(j + 1) * HEAD_DIM]
    win_ref[...] = mm(_C_WIN, _C_MISC)
    z = mm(_C_MISC, _C_END) + b_ref[...]
    lane = lax.broadcasted_iota(jnp.int32, z.shape, 1)
    log_sig = jnp.minimum(z, 0.0) - jnp.log1p(jnp.exp(-jnp.abs(z)))
    sig = 1.0 / (1.0 + jnp.exp(-z))
    misc_ref[...] = jnp.where(lane < N_FOX, log_sig, sig)


def _proj(x, g, w, b, tm=256):
    n = x.shape[0]
    tm = min(tm, n)
    row = lambda i: (i, 0)
    out_shape = (jax.ShapeDtypeStruct((n, D_FOX), BF16), jax.ShapeDtypeStruct((n, 2 * D_FOX), F32),
                 jax.ShapeDtypeStruct((n, D_NSA), BF16), jax.ShapeDtypeStruct((n, 4 * D_KV), F32),
                 jax.ShapeDtypeStruct((2 * N_KV, n, HEAD_DIM), F32), jax.ShapeDtypeStruct((n, 2 * D_KV), F32),
                 jax.ShapeDtypeStruct((n, MISC_W), F32))
    out_specs = (pl.BlockSpec((tm, D_FOX), row), pl.BlockSpec((tm, 2 * D_FOX), row),
                 pl.BlockSpec((tm, D_NSA), row), pl.BlockSpec((tm, 4 * D_KV), row),
                 pl.BlockSpec((2 * N_KV, tm, HEAD_DIM), lambda i: (0, i, 0)), pl.BlockSpec((tm, 2 * D_KV), row),
                 pl.BlockSpec((tm, MISC_W), row))
    return pl.pallas_call(
        _proj_kernel, out_shape=out_shape, grid=(n // tm,),
        in_specs=[pl.BlockSpec((tm, D_MODEL), row), _const_spec((1, D_MODEL)),
                  _const_spec((D_MODEL, _C_END)), _const_spec((1, MISC_W))],
        out_specs=out_specs, compiler_params=_cparams(("parallel",)), name="proj")(x, g, w, b)


def _cumsum_kernel(misc_ref, tri_ref, ccol_ref, crow_ref):
    hi, mid, lo = _split3(misc_ref[...])
    tri = tri_ref[...]
    c = _dot(tri, hi) + _dot(tri, mid) + _dot(tri, lo)
    ccol_ref[...] = c
    crow_ref[...] = c.T[:N_FOX, :]


def _cumsum(misc):
    b, t, _ = misc.shape
    tri = jnp.tril(jnp.ones((t, t), BF16))
    return pl.pallas_call(
        _cumsum_kernel,
        out_shape=(jax.ShapeDtypeStruct((b, t, MISC_W), F32), jax.ShapeDtypeStruct((b, N_FOX, t), F32)),
        grid=(b,),
        in_specs=[pl.BlockSpec((None, t, MISC_W), lambda i: (i, 0, 0)), _const_spec((t, t))],
        out_specs=(pl.BlockSpec((None, t, MISC_W), lambda i: (i, 0, 0)),
                   pl.BlockSpec((None, N_FOX, t), lambda i: (i, 0, 0))),
        compiler_params=_cparams(("parallel",)), name="fox_cumsum")(misc, tri)


def _fox_kernel(q_ref, k_ref, v_ref, ccol_ref, crow_ref, o_ref, kb_sc, vb_sc, ck_sc, *, tq, tk):
    hp = pl.program_id(1)
    qi = pl.program_id(2)
    t = k_ref.shape[0]
    nk = t // tk

    @pl.when(qi == 0)
    def _():
        crow = crow_ref[...]
        sub = lax.broadcasted_iota(jnp.int32, crow.shape, 0)
        for i in range(2):
            sl = slice(i * HEAD_DIM, (i + 1) * HEAD_DIM)
            kb_sc[i] = k_ref[:, sl].astype(BF16)
            vb_sc[i] = v_ref[:, sl].astype(BF16)
            ck = jnp.sum(jnp.where(sub == 2 * hp + i, crow, 0.0), axis=0, keepdims=True)
            for j in range(nk):
                ck_sc[i, j] = ck[:, j * tk:(j + 1) * tk]

    q = q_ref[...]
    ccol = ccol_ref[...]
    qpos = qi * tq + lax.broadcasted_iota(jnp.int32, (tq, 1), 0)
    outs = []
    for i in range(2):
        qh = q[:, i * HEAD_DIM:(i + 1) * HEAD_DIM]
        cq = _lane_pick(ccol, 2 * hp + i)

        def body(kj, carry, i=i, qh=qh, cq=cq):
            m, l, acc = carry
            ks = pl.multiple_of(kj * tk, tk)
            k = kb_sc[i, pl.ds(ks, tk), :]
            v = vb_sc[i, pl.ds(ks, tk), :]
            s = _dot_nt(qh, k) + (cq - ck_sc[i, kj])
            kpos = kj * tk + lax.broadcasted_iota(jnp.int32, (1, tk), 1)
            mask = kpos <= qpos
            s = jnp.where(mask, s, NEG)
            m_new = jnp.maximum(m, jnp.max(s, axis=1, keepdims=True))
            p = jnp.where(mask, jnp.exp(s - m_new), 0.0)
            alpha = jnp.exp(m - m_new)
            l = alpha * l + jnp.sum(p, axis=1, keepdims=True)
            acc = alpha * acc + _dot(p.astype(BF16), v)
            return m_new, l, acc

        init = (jnp.full((tq, 1), NEG, F32), jnp.zeros((tq, 1), F32), jnp.zeros((tq, HEAD_DIM), F32))
        n_chunks = (qi * tq + tq + tk - 1) // tk
        _, l, acc = lax.fori_loop(0, n_chunks, body, init)
        outs.append(acc / jnp.maximum(l, 1e-30))
    o_ref[...] = jnp.concatenate(outs, axis=1).astype(BF16)


def _fox_prompt(qf, fkv, ccol, crow, tq=256, tk=256):
    b, t, _ = qf.shape
    tq = min(tq, t)
    tk = min(tk, t)
    nhp = N_FOX // 2
    return pl.pallas_call(
        functools.partial(_fox_kernel, tq=tq, tk=tk),
        out_shape=jax.ShapeDtypeStruct((b, t, D_FOX), BF16),
        grid=(b, nhp, t // tq),
        in_specs=[pl.BlockSpec((None, tq, LANES), lambda bi, hp, qi: (bi, qi, hp)),
                  pl.BlockSpec((None, t, LANES), lambda bi, hp, qi: (bi, 0, hp)),
                  pl.BlockSpec((None, t, LANES), lambda bi, hp, qi: (bi, 0, nhp + hp)),
                  pl.BlockSpec((None, tq, MISC_W), lambda bi, hp, qi: (bi, qi, 0)),
                  pl.BlockSpec((None, N_FOX, t), lambda bi, hp, qi: (bi, 0, 0))],
        out_specs=pl.BlockSpec((None, tq, LANES), lambda bi, hp, qi: (bi, qi, hp)),
        scratch_shapes=[pltpu.VMEM((2, t, HEAD_DIM), BF16), pltpu.VMEM((2, t, HEAD_DIM), BF16),
                        pltpu.VMEM((2, t // tk, 1, tk), F32)],
        compiler_params=_cparams(("parallel", "parallel", "arbitrary")), name="fox_prompt")(qf, fkv, fkv, ccol, crow)


def _compress_rows(x, pe_ref, w1_ref, w2_ref):
    r = x.shape[0]
    xa = (x + pe_ref[0:1, :]).astype(BF16)
    xb = (x + pe_ref[1:2, :]).astype(BF16)
    ha = _dot(xa, w1_ref[:CMP_ROW, :])
    hb = _dot(xb, w1_ref[CMP_ROW:, :])
    hid = ha + pltpu.roll(hb, r - 1, 0)
    gl = 0.5 * hid * (1.0 + jnp.tanh(0.7978845608028654 * (hid + 0.044715 * (hid * hid * hid))))
    return _dot(gl.astype(BF16), w2_ref[...])


def _compress_kernel(x_ref, pe_ref, w1_ref, w2_ref, o_ref):
    o_ref[...] = _compress_rows(x_ref[...], pe_ref, w1_ref, w2_ref).astype(BF16)


def _compress_prompt(kcvc, pe2, w1, w2, b, t):
    r = t // STRIDE
    x = kcvc.reshape(2 * N_KV, b * r, CMP_ROW)
    return pl.pallas_call(
        _compress_kernel,
        out_shape=jax.ShapeDtypeStruct((b, 2 * N_KV, r, HEAD_DIM), BF16),
        grid=(b, 2 * N_KV),
        in_specs=[pl.BlockSpec((None, r, CMP_ROW), lambda bi, j: (j, bi, 0)),
                  pl.BlockSpec((None, 2, CMP_ROW), lambda bi, j: (j // N_KV, 0, 0)),
                  pl.BlockSpec((None, L_CMP * HEAD_DIM, LANES), lambda bi, j: (j // N_KV, 0, 0)),
                  pl.BlockSpec((None, LANES, HEAD_DIM), lambda bi, j: (j // N_KV, 0, 0))],
        out_specs=pl.BlockSpec((None, None, r, HEAD_DIM), lambda bi, j: (bi, j, 0, 0)),
        compiler_params=_cparams(("parallel", "parallel")), name="nsa_compress")(x, pe2, w1, w2)


def _slope_col(slopes_ref, g, hrow):
    s0, s1, s2, s3 = (slopes_ref[g, h] for h in range(HPG))
    return jnp.where(hrow == 0, s0, jnp.where(hrow == 1, s1, jnp.where(hrow == 2, s2, s3)))


def _select_blocks(p_slc, cur, nsb):
    jb = lax.broadcasted_iota(jnp.int32, p_slc.shape, 1)
    valid = jb <= cur
    forced = valid & ((jb == 0) | ((cur - jb) < N_LOCAL))
    score = jnp.where(valid, p_slc, -jnp.inf)
    score = jnp.where(forced, jnp.inf, score)
    rank = jnp.zeros(p_slc.shape, jnp.int32)
    for i in range(nsb):
        ci = score[:, i:i + 1]
        beats = (ci > score) | ((ci == score) & (jb > i))
        rank = rank + jnp.where(beats, 1, 0)
    return jnp.where((rank < N_SEL) & (jb < nsb), 1.0, 0.0)


def _cmp_branch(q4, kc, vc, slope, qpos, tq, m_ref):
    ncp = kc.shape[0]
    s = _dot_nt(q4, kc)
    cend = lax.broadcasted_iota(jnp.int32, (1, ncp), 1) * STRIDE + (L_CMP - 1)
    s = s - slope * (qpos.astype(F32) - cend.astype(F32))
    pc = _masked_softmax(s, cend <= qpos)
    o_c = _dot(pc.astype(BF16), vc)
    pg = pc[0:tq] + pc[tq:2 * tq] + pc[2 * tq:3 * tq] + pc[3 * tq:4 * tq]
    return o_c, _dot3(pg, m_ref[...])


def _gate_col(misc, br, g):
    cols = [_lane_pick(misc, GATE_OFF + br * N_NSA + g * HPG + h) for h in range(HPG)]
    return jnp.concatenate(cols, axis=0)


def _slc_matrix(nc, nsb, ncp, width):
    r, c = L_SEL // STRIDE, L_CMP // STRIDE
    offs, counts = np.unique((np.arange(r)[:, None] - np.arange(c)[None, :]).ravel(), return_counts=True)
    m = np.zeros((ncp, width), np.float32)
    for j in range(nsb):
        for off, cnt in zip(offs, counts):
            n = r * j + off
            if 0 <= n < nc:
                m[n, j] += cnt
    return jnp.asarray(m, BF16)


def _alibi_slopes():
    base = 2.0 ** (-8.0 * np.arange(1, N_NSA + 1, dtype=np.float32) / N_NSA)
    return jnp.asarray(base.astype(np.float32).reshape(HPG, N_KV).T)


def _nsa_kernel(slopes_ref, q_ref, misc_ref, cmp_ref, ks_ref, vs_ref, kw_ref, vw_ref, m_ref, e_ref, o_ref,
                ksb, vsb, kwb, vwb, *, tq, tk, nsb):
    gp = pl.program_id(1)
    qi = pl.program_id(2)
    t = ks_ref.shape[0]

    @pl.when(qi == 0)
    def _():
        for gi in range(2):
            sl = slice(gi * HEAD_DIM, (gi + 1) * HEAD_DIM)
            ksb[gi] = ks_ref[:, sl].astype(BF16)
            vsb[gi] = vs_ref[:, sl].astype(BF16)
            kwb[gi] = kw_ref[:, sl].astype(BF16)
            vwb[gi] = vw_ref[:, sl].astype(BF16)

    a = qi * tq
    rows = HPG * tq
    row = lax.broadcasted_iota(jnp.int32, (rows, 1), 0)
    qpos = a + row % tq
    qposf = qpos.astype(F32)
    hrow = row // tq
    misc = misc_ref[...]
    cur = (a + lax.broadcasted_iota(jnp.int32, (tq, 1), 0)) // L_SEL
    lw = min(WINDOW + tq, t)
    outs = []
    for gi in range(2):
        g = 2 * gp + gi
        slope = _slope_col(slopes_ref, g, hrow)
        q4 = jnp.concatenate(
            [q_ref[:, (gi * HPG + h) * HEAD_DIM:(gi * HPG + h + 1) * HEAD_DIM] for h in range(HPG)], axis=0)
        o_c, p_slc = _cmp_branch(q4, cmp_ref[g], cmp_ref[N_KV + g], slope, qpos, tq, m_ref)
        sel = _select_blocks(p_slc, cur, nsb).astype(BF16)

        def body(kj, carry, gi=gi, q4=q4, slope=slope, sel=sel):
            m, l, acc = carry
            ks = pl.multiple_of(kj * tk, tk)
            k = ksb[gi, pl.ds(ks, tk), :]
            v = vsb[gi, pl.ds(ks, tk), :]
            kpos = kj * tk + lax.broadcasted_iota(jnp.int32, (1, tk), 1)
            s = _dot_nt(q4, k) - slope * (qposf - kpos.astype(F32))
            selx = _dot(sel, e_ref[kj])
            selx = jnp.concatenate([selx] * HPG, axis=0)
            mask = (selx > 0.5) & (kpos <= qpos)
            s = jnp.where(mask, s, NEG)
            m_new = jnp.maximum(m, jnp.max(s, axis=1, keepdims=True))
            p = jnp.where(mask, jnp.exp(s - m_new), 0.0)
            alpha = jnp.exp(m - m_new)
            l = alpha * l + jnp.sum(p, axis=1, keepdims=True)
            acc = alpha * acc + _dot(p.astype(BF16), v)
            return m_new, l, acc

        init = (jnp.full((rows, 1), NEG, F32), jnp.zeros((rows, 1), F32), jnp.zeros((rows, HEAD_DIM), F32))
        n_chunks = (a + tq + tk - 1) // tk
        _, l, acc = lax.fori_loop(0, n_chunks, body, init)
        o_s = acc / jnp.maximum(l, 1e-30)

        start = pl.multiple_of(jnp.maximum(a + tq - lw, 0), 2 * SUBLANES)
        kw = kwb[gi, pl.ds(start, lw), :]
        vw = vwb[gi, pl.ds(start, lw), :]
        kpos = start + lax.broadcasted_iota(jnp.int32, (1, lw), 1)
        dist = qpos - kpos
        s = _dot_nt(q4, kw) - slope * dist.astype(F32)
        pw = _masked_softmax(s, (dist >= 0) & (dist < WINDOW))
        o_w = _dot(pw.astype(BF16), vw)

        o = _gate_col(misc, 0, g) * o_c + _gate_col(misc, 1, g) * o_s + _gate_col(misc, 2, g) * o_w
        outs += [o[h * tq:(h + 1) * tq] for h in range(HPG)]
    o_ref[...] = jnp.concatenate(outs, axis=1).astype(BF16)


def _nsa_prompt(qn, misc, cmp, nkv, win, tq=128):
    b, t, _ = qn.shape
    tq = min(tq, t)
    tk = tq
    ncp = t // STRIDE
    nc = (t - L_CMP) // STRIDE + 1
    nsb = -(-t // L_SEL)
    m = _slc_matrix(nc, nsb, ncp, LANES)
    e = (np.arange(t)[None, :] // L_SEL == np.arange(LANES)[:, None]).astype(np.float32)
    e = jnp.asarray(e.reshape(LANES, t // tk, tk).transpose(1, 0, 2), BF16)
    pair = 2 * HEAD_DIM
    grid_spec = pl.GridSpec(
        grid=(b, N_KV // 2, t // tq),
        in_specs=[pl.BlockSpec(memory_space=pltpu.SMEM),
                  pl.BlockSpec((None, tq, 2 * HPG * HEAD_DIM), lambda bi, gp, qi: (bi, qi, gp)),
                  pl.BlockSpec((None, tq, MISC_W), lambda bi, gp, qi: (bi, qi, 0)),
                  pl.BlockSpec((None, 2 * N_KV, ncp, HEAD_DIM), lambda bi, gp, qi: (bi, 0, 0, 0)),
                  pl.BlockSpec((None, t, pair), lambda bi, gp, qi: (bi, 0, 4 + gp)),
                  pl.BlockSpec((None, t, pair), lambda bi, gp, qi: (bi, 0, 6 + gp)),
                  pl.BlockSpec((None, t, pair), lambda bi, gp, qi: (bi, 0, gp)),
                  pl.BlockSpec((None, t, pair), lambda bi, gp, qi: (bi, 0, 2 + gp)),
                  _const_spec((ncp, LANES)), _const_spec((t // tk, LANES, tk))],
        out_specs=pl.BlockSpec((None, tq, 2 * HPG * HEAD_DIM), lambda bi, gp, qi: (bi, qi, gp)),
        scratch_shapes=[pltpu.VMEM((2, t, HEAD_DIM), BF16) for _ in range(4)])
    return pl.pallas_call(
        functools.partial(_nsa_kernel, tq=tq, tk=tk, nsb=nsb),
        out_shape=jax.ShapeDtypeStruct((b, t, D_NSA), BF16), grid_spec=grid_spec,
        compiler_params=_cparams(("parallel", "parallel", "arbitrary")), name="nsa_prompt",
    )(_alibi_slopes(), qn, misc, cmp, nkv, nkv, win, win, m, e)


def _fox_dec_kernel(pt_ref, q_ref, kv_ref, lf_ref, kvn_ref, lfn_ref, ex_ref, ut_ref, o_ref,
                    qbd_sc, pad_sc, m_sc, l_sc, r_sc, cn_sc, acc_sc, *, n_new):
    p = pl.program_id(1)
    page = kv_ref.shape[0]
    cols = n_new * N_FOX

    def accumulate(kb, vb, s_bias, mask):
        s = _dot_nt(kb, qbd_sc[...]) + s_bias
        if mask is not None:
            s = jnp.where(mask, s, NEG)
        m_old = m_sc[...]
        m_new = jnp.maximum(m_old, jnp.max(s, axis=0, keepdims=True))
        pr = jnp.exp(s - m_new)
        if mask is not None:
            pr = jnp.where(mask, pr, 0.0)
        alpha = jnp.exp(m_old - m_new)
        l_sc[...] = alpha * l_sc[...] + jnp.sum(pr, axis=0, keepdims=True)
        m_sc[...] = m_new
        acc_sc[...] = _row_to_col(alpha) * acc_sc[...] + _dot(pr.T.astype(BF16), vb)

    @pl.when(p == 0)
    def _():
        q = q_ref[...]
        rep = jnp.concatenate([jnp.broadcast_to(q[t:t + 1, :], (N_FOX, D_FOX)) for t in range(n_new)]
                              + [jnp.zeros((LANES - cols, D_FOX), F32)], axis=0)
        rr = lax.broadcasted_iota(jnp.int32, (LANES, D_FOX), 0)
        ll = lax.broadcasted_iota(jnp.int32, (LANES, D_FOX), 1)
        qbd_sc[...] = jnp.where((ll // HEAD_DIM == rr % N_FOX) & (rr < cols), rep, 0.0).astype(BF16)
        lfe = _dot3(lfn_ref[...], ex_ref[...])
        lane_t = lax.broadcasted_iota(jnp.int32, (1, LANES), 1) // N_FOX
        tkey = lax.broadcasted_iota(jnp.int32, (page, LANES), 0)
        run = jnp.zeros((1, LANES), F32)
        cn = jnp.zeros((1, LANES), F32)
        ck = jnp.zeros((page, LANES), F32)
        for t in range(n_new):
            run = run + lfe[t:t + 1, :]
            cn = jnp.where(lane_t == t, run, cn)
            ck = jnp.where(tkey == t, run, ck)
        cn_sc[...] = cn
        m_sc[...] = jnp.full((1, LANES), NEG, F32)
        l_sc[...] = jnp.zeros((1, LANES), F32)
        r_sc[...] = jnp.zeros((1, LANES), F32)
        acc_sc[...] = jnp.zeros_like(acc_sc)
        pad_sc[...] = jnp.zeros_like(pad_sc)
        pad_sc[0:SQ, :] = kvn_ref[...]
        kvn = pad_sc[...]
        tq_ = lax.broadcasted_iota(jnp.int32, (page, LANES), 1) // N_FOX
        mask = (tkey <= tq_) & (tkey < n_new) & (tq_ < n_new)
        accumulate(kvn[:, :D_FOX].astype(BF16), kvn[:, D_FOX:].astype(BF16), cn - ck, mask)

    kv = kv_ref[...]
    lfx = _dot3(lf_ref[...], ex_ref[0:N_FOX, :])
    hi, mid, lo = _split3(lfx)
    ut = ut_ref[...]
    suffix = _dot(ut, hi) + _dot(ut, mid) + _dot(ut, lo) + r_sc[...]
    r_sc[...] = r_sc[...] + jnp.sum(lfx, axis=0, keepdims=True)
    accumulate(kv[:, :D_FOX].astype(BF16), kv[:, D_FOX:].astype(BF16), cn_sc[...] + suffix, None)

    @pl.when(p == pl.num_programs(1) - 1)
    def _():
        o = acc_sc[...] / jnp.maximum(_row_to_col(l_sc[...]), 1e-30)
        rr = lax.broadcasted_iota(jnp.int32, o.shape, 0)
        ll = lax.broadcasted_iota(jnp.int32, o.shape, 1)
        o = jnp.where(ll // HEAD_DIM == rr % N_FOX, o, 0.0)
        orow = lax.broadcasted_iota(jnp.int32, (SQ, D_FOX), 0)
        out = jnp.zeros((SQ, D_FOX), F32)
        for t in range(n_new):
            out = jnp.where(orow == t, jnp.sum(o[t * N_FOX:(t + 1) * N_FOX], axis=0, keepdims=True), out)
        o_ref[...] = out


def _fox_decode(pt_flat, qf_pad, cache_kv, cache_lf, fkv_pad, misc_pad, n_pages, n_new):
    db = qf_pad.shape[0]
    page = cache_kv.shape[1]
    ex = np.zeros((LANES, LANES), np.float32)
    for t in range(n_new):
        ex[np.arange(N_FOX), t * N_FOX + np.arange(N_FOX)] = 1.0
    ut = np.triu(np.ones((page, page), np.float32), 1)
    pidx = lambda b, p, pt: (pt[b * n_pages + n_pages - 1 - p], 0, 0)
    per_b = lambda b, p, pt: (b, 0, 0)
    grid_spec = pltpu.PrefetchScalarGridSpec(
        num_scalar_prefetch=1, grid=(db, n_pages),
        in_specs=[pl.BlockSpec((None, SQ, D_FOX), per_b),
                  pl.BlockSpec((None, page, 2 * D_FOX), pidx),
                  pl.BlockSpec((None, page, N_FOX), pidx),
                  pl.BlockSpec((None, SQ, 2 * D_FOX), per_b),
                  pl.BlockSpec((None, SQ, MISC_W), per_b),
                  _const_spec((LANES, LANES)), _const_spec((page, page))],
        out_specs=pl.BlockSpec((None, SQ, D_FOX), per_b),
        scratch_shapes=[pltpu.VMEM((LANES, D_FOX), BF16), pltpu.VMEM((page, 2 * D_FOX), F32),
                        pltpu.VMEM((1, LANES), F32), pltpu.VMEM((1, LANES), F32), pltpu.VMEM((1, LANES), F32),
                        pltpu.VMEM((1, LANES), F32), pltpu.VMEM((LANES, D_FOX), F32)])
    return pl.pallas_call(
        functools.partial(_fox_dec_kernel, n_new=n_new),
        out_shape=jax.ShapeDtypeStruct((db, SQ, D_FOX), F32), grid_spec=grid_spec,
        compiler_params=_cparams(("parallel", "arbitrary")), name="fox_decode",
    )(pt_flat, qf_pad, cache_kv, cache_lf, fkv_pad, misc_pad, jnp.asarray(ex, BF16), jnp.asarray(ut, BF16))


def _nsa_pages_kernel(pt_ref, pg_ref, new_ref, pe_ref, w1_ref, w2_ref, ksvs_ref, cmp_ref, x_sc):
    p = pl.program_id(1)
    npg = pl.num_programs(1) - 1
    cpp = 4 * D_KV // LANES
    page = pg_ref.shape[0] // cpp
    rpp = page // STRIDE
    half = cpp // 2

    @pl.when(p < npg)
    def _():
        ksvs_ref[...] = jnp.concatenate(
            [pg_ref[pl.ds(c, page, stride=cpp), :] for c in range(half, cpp)], axis=1).astype(BF16)
        r0 = pl.multiple_of(p * rpp, rpp)
        for l in range(STRIDE):
            for c in range(half):
                rows = pg_ref[pl.ds(l * cpp + c, rpp, stride=STRIDE * cpp), :]
                for i in range(2):
                    x_sc[2 * c + i, pl.ds(r0, rpp), l * HEAD_DIM:(l + 1) * HEAD_DIM] = (
                        rows[:, i * HEAD_DIM:(i + 1) * HEAD_DIM])

    @pl.when(p == npg)
    def _():
        new = new_ref[:, 2 * D_KV:]
        ksvs_ref[...] = jnp.concatenate([new, jnp.zeros((page - SQ, 2 * D_KV), F32)], axis=0).astype(BF16)
        for j in range(2 * N_KV):
            s = j // N_KV
            cmp_ref[j] = _compress_rows(x_sc[j], pe_ref.at[s], w1_ref.at[s], w2_ref.at[s]).astype(BF16)


def _nsa_pages(pt_flat, cache_nsa, nkv_pad, pe2, w1, w2, n_pages):
    db = nkv_pad.shape[0]
    n_pool, page, _ = cache_nsa.shape
    past = n_pages * page
    rows = past // STRIDE
    cpp = 4 * D_KV // LANES
    cache_nsa = cache_nsa.reshape(n_pool, page * cpp, LANES)
    pidx = lambda b, p, pt: (pt[b * n_pages + jnp.minimum(p, n_pages - 1)], 0, 0)
    grid_spec = pltpu.PrefetchScalarGridSpec(
        num_scalar_prefetch=1, grid=(db, n_pages + 1),
        in_specs=[pl.BlockSpec((None, page * cpp, LANES), pidx),
                  pl.BlockSpec((None, SQ, 4 * D_KV), lambda b, p, pt: (b, 0, 0)),
                  _const_spec((2, 2, CMP_ROW)), _const_spec((2, L_CMP * HEAD_DIM, LANES)),
                  _const_spec((2, LANES, HEAD_DIM))],
        out_specs=(pl.BlockSpec((None, page, 2 * D_KV), lambda b, p, pt: (b, p, 0)),
                   pl.BlockSpec((None, 2 * N_KV, rows, HEAD_DIM), lambda b, p, pt: (b, 0, 0, 0))),
        scratch_shapes=[pltpu.VMEM((2 * N_KV, rows, CMP_ROW), F32)])
    return pl.pallas_call(
        _nsa_pages_kernel,
        out_shape=(jax.ShapeDtypeStruct((db, past + page, 2 * D_KV), BF16),
                   jax.ShapeDtypeStruct((db, 2 * N_KV, rows, HEAD_DIM), BF16)),
        grid_spec=grid_spec, compiler_params=_cparams(("parallel", "arbitrary")), name="nsa_pages",
    )(pt_flat, cache_nsa, nkv_pad, pe2, w1, w2)


def _nsa_dec_kernel(slopes_ref, q_ref, misc_ref, cmp_ref, ksvs_ref, win_ref, winn_ref, m_ref, e_ref, o_ref,
                    selx_sc, *, past, nsb):
    tq = SQ
    rows = HPG * tq
    row = lax.broadcasted_iota(jnp.int32, (rows, 1), 0)
    qpos = past + row % tq
    qposf = qpos.astype(F32)
    hrow = row // tq
    misc = misc_ref[...]
    q = q_ref[...]
    cur = (past + lax.broadcasted_iota(jnp.int32, (tq, 1), 0)) // L_SEL
    nkeys = ksvs_ref.shape[0]

    o_cs, sels, q4s, slopes = [], [], [], []
    for g in range(N_KV):
        slope = _slope_col(slopes_ref, g, hrow)
        q4 = jnp.concatenate(
            [q[:, (g * HPG + h) * HEAD_DIM:(g * HPG + h + 1) * HEAD_DIM] for h in range(HPG)], axis=0).astype(BF16)
        o_c, p_slc = _cmp_branch(q4, cmp_ref[g], cmp_ref[N_KV + g], slope, qpos, tq, m_ref)
        sels.append(_select_blocks(p_slc, cur, nsb))
        o_cs.append(o_c)
        q4s.append(q4)
        slopes.append(slope)
    selx_sc[...] = _dot(jnp.concatenate(sels, axis=0).astype(BF16), e_ref[...])

    kpos = lax.broadcasted_iota(jnp.int32, (1, nkeys), 1)
    nwin = win_ref.shape[0]
    wpos = (past - nwin) + lax.broadcasted_iota(jnp.int32, (1, nwin + winn_ref.shape[0]), 1)
    outs = []
    for g in range(N_KV):
        q4, slope = q4s[g], slopes[g]
        ks = ksvs_ref[:, g * HEAD_DIM:(g + 1) * HEAD_DIM]
        vs = ksvs_ref[:, D_KV + g * HEAD_DIM:D_KV + (g + 1) * HEAD_DIM]
        s = _dot_nt(q4, ks) - slope * (qposf - kpos.astype(F32))
        selx = selx_sc[g * tq:(g + 1) * tq, :]
        mask = (jnp.concatenate([selx] * HPG, axis=0) > 0.5) & (kpos <= qpos)
        o_s = _dot(_masked_softmax(s, mask).astype(BF16), vs)
        kw = jnp.concatenate([win_ref[:, g * HEAD_DIM:(g + 1) * HEAD_DIM],
                              winn_ref[:, g * HEAD_DIM:(g + 1) * HEAD_DIM]], axis=0).astype(BF16)
        vw = jnp.concatenate([win_ref[:, D_KV + g * HEAD_DIM:D_KV + (g + 1) * HEAD_DIM],
                              winn_ref[:, D_KV + g * HEAD_DIM:D_KV + (g + 1) * HEAD_DIM]], axis=0).astype(BF16)
        dist = qpos - wpos
        s = _dot_nt(q4, kw) - slope * dist.astype(F32)
        o_w = _dot(_masked_softmax(s, (dist >= 0) & (dist < WINDOW)).astype(BF16), vw)
        o = _gate_col(misc, 0, g) * o_cs[g] + _gate_col(misc, 1, g) * o_s + _gate_col(misc, 2, g) * o_w
        outs += [o[h * tq:(h + 1) * tq] for h in range(HPG)]
    o_ref[...] = jnp.concatenate(outs, axis=1)


def _nsa_decode(qn_pad, misc_pad, cmp, ksvs, win_state, win_new_pad, past, n_new):
    db = qn_pad.shape[0]
    nkeys = ksvs.shape[1]
    ncp = cmp.shape[2]
    tk_all = past + n_new
    nc = (tk_all - L_CMP) // STRIDE + 1
    nsb = -(-tk_all // L_SEL)
    width = -(-nsb // LANES) * LANES
    m = _slc_matrix(nc, nsb, ncp, width)
    e = jnp.asarray(np.arange(nkeys)[None, :] // L_SEL == np.arange(width)[:, None], BF16)
    nwin = win_state.shape[1]
    nwn = win_new_pad.shape[1]
    per_b3 = lambda b: (b, 0, 0)
    grid_spec = pl.GridSpec(
        grid=(db,),
        in_specs=[pl.BlockSpec(memory_space=pltpu.SMEM),
                  pl.BlockSpec((None, SQ, D_NSA), per_b3), pl.BlockSpec((None, SQ, MISC_W), per_b3),
                  pl.BlockSpec((None, 2 * N_KV, ncp, HEAD_DIM), lambda b: (b, 0, 0, 0)),
                  pl.BlockSpec((None, nkeys, 2 * D_KV), per_b3),
                  pl.BlockSpec((None, nwin, 2 * D_KV), per_b3), pl.BlockSpec((None, nwn, 2 * D_KV), per_b3),
                  _const_spec((ncp, width)), _const_spec((width, nkeys))],
        out_specs=pl.BlockSpec((None, SQ, D_NSA), per_b3),
        scratch_shapes=[pltpu.VMEM((N_KV * SQ, nkeys), F32)])
    return pl.pallas_call(
        functools.partial(_nsa_dec_kernel, past=past, nsb=nsb),
        out_shape=jax.ShapeDtypeStruct((db, SQ, D_NSA), F32), grid_spec=grid_spec,
        compiler_params=_cparams(("parallel",)), name="nsa_decode",
    )(_alibi_slopes(), qn_pad, misc_pad, cmp, ksvs, win_state, win_new_pad, m, e)


def _prep_w_in(w_in, b_forget, b_gate):
    sizes = (D_FOX, D_FOX, D_FOX, N_FOX, D_NSA) + (D_KV,) * 6 + (3 * N_NSA,)
    offs = np.cumsum((0,) + sizes)
    part = lambda i, j=None: w_in[:, offs[i]:offs[(i if j is None else j) + 1]]
    pad = MISC_W - N_FOX - 3 * N_NSA
    w = jnp.concatenate([part(0), part(1, 2), part(4), part(5, 10), part(3), part(11),
                         jnp.zeros((w_in.shape[0], pad), w_in.dtype)], axis=1).astype(BF16)
    b = jnp.concatenate([b_forget, b_gate, jnp.zeros((pad,), F32)]).reshape(1, MISC_W).astype(F32)
    return w, b


def kernel(x_prompt, x_sample, cache_fox_kv, cache_fox_logf, cache_nsa_kv, state_win_kv, page_table, norm_ffn1,
           ffn1_gate, ffn1_up, ffn1_down, norm_mix, w_in, b_forget, b_gate, phi_pe, phi_w1, phi_w2, w_out,
           norm_ffn2, ffn2_gate, ffn2_up, ffn2_down, norm_final):
    depth = norm_ffn1.shape[0]
    bsz, t, _ = x_prompt.shape
    db, s_new, _ = x_sample.shape
    n_pool, page = cache_fox_kv.shape[1], cache_fox_kv.shape[2]
    n_pages = page_table.shape[1]
    past = n_pages * page
    wb = state_win_kv.shape[2]
    wp = min(WINDOW, t)
    assert s_new <= SQ and wb == min(WINDOW, past)

    xp = x_prompt.reshape(bsz * t, D_MODEL)
    xs = x_sample.reshape(db * s_new, D_MODEL)
    fox_kv_pool = cache_fox_kv.reshape(depth * n_pool, page, 2 * D_FOX)
    fox_lf_pool = cache_fox_logf.reshape(depth * n_pool, page, N_FOX)
    nsa_kv_pool = cache_nsa_kv.reshape(depth * n_pool, page, 4 * D_KV)
    gfin = norm_final.reshape(1, D_MODEL)
    outs = [[] for _ in range(8)]

    def pad_rows(a, rows):
        a = a.reshape(db, s_new, -1).astype(F32)
        return jnp.pad(a, ((0, 0), (0, rows - s_new), (0, 0)))

    for l in range(depth):
        last = l == depth - 1
        g1 = norm_ffn1[l].reshape(1, D_MODEL)
        gm = norm_mix[l].reshape(1, D_MODEL)
        g2 = norm_ffn2[l].reshape(1, D_MODEL)
        ffn1 = (ffn1_gate[l].astype(BF16), ffn1_up[l].astype(BF16), ffn1_down[l].astype(BF16))
        ffn2 = (ffn2_gate[l].astype(BF16), ffn2_up[l].astype(BF16), ffn2_down[l].astype(BF16))
        w_r, b_r = _prep_w_in(w_in[l], b_forget[l], b_gate[l])
        wo = w_out[l].astype(BF16)
        pe2 = phi_pe[l].reshape(2, 2, CMP_ROW)
        w1 = phi_w1[l].astype(BF16)
        w2 = phi_w2[l].astype(BF16)
        pt_flat = (page_table + l * n_pool).reshape(-1).astype(jnp.int32)

        xp = _ffn(xp, g1, *ffn1)
        xs = _ffn(xs, g1, *ffn1)

        qf, fkv, qn, nkv, kcvc, win, misc = _proj(xp, gm, w_r, b_r)
        r3 = lambda a: a.reshape(bsz, t, -1)
        ccol, crow = _cumsum(r3(misc))
        o_f = _fox_prompt(r3(qf), r3(fkv), ccol, crow)
        cmp = _compress_prompt(kcvc, pe2, w1, w2, bsz, t)
        o_n = _nsa_prompt(r3(qn), r3(misc), cmp, r3(nkv), r3(win))
        xp = _ffn(xp, g2, *ffn2, mix=(o_f.reshape(bsz * t, D_FOX), o_n.reshape(bsz * t, D_NSA), wo),
                  gf=gfin if last else None)
        outs[0].append(fkv.reshape(bsz, t, 2, N_FOX, HEAD_DIM))
        outs[1].append(r3(misc)[:, :, :N_FOX])
        outs[2].append(nkv.reshape(bsz, t, 4, N_KV, HEAD_DIM))
        outs[3].append(win.reshape(bsz, t, 2, N_KV, HEAD_DIM)[:, t - wp:])

        qf, fkv, qn, nkv, _, win, misc = _proj(xs, gm, w_r, b_r)
        o_f = _fox_decode(pt_flat, pad_rows(qf, SQ), fox_kv_pool, fox_lf_pool, pad_rows(fkv, SQ),
                          pad_rows(misc, SQ), n_pages, s_new)
        ksvs, cmp = _nsa_pages(pt_flat, nsa_kv_pool, pad_rows(nkv, SQ), pe2, w1, w2, n_pages)
        o_n = _nsa_decode(pad_rows(qn, SQ), pad_rows(misc, SQ), cmp, ksvs, state_win_kv[l].reshape(db, wb, 2 * D_KV),
                          pad_rows(win, page), past, s_new)
        o_f = o_f[:, :s_new].reshape(db * s_new, D_FOX).astype(BF16)
        o_n = o_n[:, :s_new].reshape(db * s_new, D_NSA).astype(BF16)
        xs = _ffn(xs, g2, *ffn2, mix=(o_f, o_n, wo), gf=gfin if last else None)
        win_new = win.reshape(db, s_new, 2, N_KV, HEAD_DIM)
        outs[4].append(fkv.reshape(db, s_new, 2, N_FOX, HEAD_DIM))
        outs[5].append(misc.reshape(db, s_new, MISC_W)[:, :, :N_FOX])
        outs[6].append(nkv.reshape(db, s_new, 4, N_KV, HEAD_DIM))
        outs[7].append(jnp.concatenate([state_win_kv[l], win_new], axis=1)[:, -wb:])

    y_prompt = xp.reshape(bsz, t, D_MODEL)
    y_sample = xs.reshape(db, s_new, D_MODEL)
    return (y_prompt, y_sample) + tuple(jnp.stack(o) for o in outs)
```

```python
import functools

import numpy as np
import jax
import jax.numpy as jnp
from jax import lax
from jax.experimental import pallas as pl
from jax.experimental.pallas import tpu as pltpu

F32 = jnp.float32
BF16 = jnp.bfloat16

D_MODEL = 2048
HEAD_DIM = 64
N_FOX = 16
N_NSA = 16
N_KV = 4
HPG = N_NSA // N_KV
D_FOX = N_FOX * HEAD_DIM
D_NSA = N_NSA * HEAD_DIM
D_KV = N_KV * HEAD_DIM
D_FF = 5632
L_CMP = 32
STRIDE = 16
L_SEL = 64
N_SEL = 16
N_LOCAL = 2
WINDOW = 512
RMS_EPS = 1e-6
NEG = -1e30
SCALE = HEAD_DIM ** -0.5

LANES = 128
SUBLANES = 8
MISC_W = LANES
GATE_OFF = N_FOX
SQ = SUBLANES
CMP_ROW = STRIDE * HEAD_DIM

_A_QF = 0
_A_QN = _A_QF + D_FOX
_A_KCVC = _A_QN + D_NSA
_A_MISC = _A_KCVC + 2 * D_KV
_A_END = _A_MISC + MISC_W
_B_FKV = 0
_B_NKV = _B_FKV + 2 * D_FOX
_B_WIN = _B_NKV + 4 * D_KV
_B_MISC = _B_WIN + 2 * D_KV
_B_END = _B_MISC + MISC_W

_VMEM_LIMIT = 56 * 1024 * 1024


def _cparams(sem):
    return pltpu.CompilerParams(dimension_semantics=sem, vmem_limit_bytes=_VMEM_LIMIT)


def _const_spec(shape):
    nd = len(shape)
    return pl.BlockSpec(shape, lambda *_: (0,) * nd, pipeline_mode=pl.Buffered(1))


def _dot(a, b):
    return jnp.dot(a, b, preferred_element_type=F32)


def _dot_nt(a, b):
    return lax.dot_general(a, b, (((1,), (1,)), ((), ())), preferred_element_type=F32)


def _split3(x):
    hi = x.astype(BF16)
    r1 = x - hi.astype(F32)
    mid = r1.astype(BF16)
    lo = (r1 - mid.astype(F32)).astype(BF16)
    return hi, mid, lo


def _dot3(x, m):
    hi, mid, lo = _split3(x)
    return _dot(hi, m) + _dot(mid, m) + _dot(lo, m)


def _rms(x, g):
    return x * lax.rsqrt(jnp.mean(x * x, axis=-1, keepdims=True) + RMS_EPS) * g


def _masked_softmax(s, mask):
    s = jnp.where(mask, s, NEG)
    e = jnp.where(mask, jnp.exp(s - jnp.max(s, axis=-1, keepdims=True)), 0.0)
    return e / jnp.maximum(jnp.sum(e, axis=-1, keepdims=True), 1e-30)


def _online_softmax_step(s, m, l):
    m_new = jnp.maximum(m, jnp.max(s, axis=1, keepdims=True))
    p = jnp.exp(s - m_new)
    alpha = jnp.exp(m - m_new)
    return m_new, alpha, alpha * l + jnp.sum(p, axis=1, keepdims=True), p


def _lane_pick(x, idx):
    lane = lax.broadcasted_iota(jnp.int32, x.shape, 1)
    return jnp.sum(jnp.where(lane == idx, x, 0.0), axis=1, keepdims=True)


def _log_sigmoid(z):
    return jnp.minimum(z, 0.0) - jnp.log1p(jnp.exp(-jnp.abs(z)))


def _sigmoid(z):
    return 1.0 / (1.0 + jnp.exp(-z))


def _ffn_kernel(*refs, has_mix, final):
    it = iter(refs)
    x_ref, g_ref, wg_ref, wu_ref, wd_ref = (next(it) for _ in range(5))
    if has_mix:
        of_ref, on_ref, wo_ref = (next(it) for _ in range(3))
    if final:
        gf_ref = next(it)
    o_ref, h_sc, acc_sc = (next(it) for _ in range(3))
    j = pl.program_id(1)

    @pl.when(j == 0)
    def _():
        x = x_ref[...]
        if has_mix:
            x = x + _dot(of_ref[...], wo_ref[:D_FOX, :]) + _dot(on_ref[...], wo_ref[D_FOX:, :])
            o_ref[...] = x
        h_sc[...] = _rms(x, g_ref[...]).astype(BF16)
        acc_sc[...] = jnp.zeros_like(acc_sc)

    h = h_sc[...]
    gate = _dot(h, wg_ref[...])
    up = _dot(h, wu_ref[...])
    a = gate * _sigmoid(gate) * up
    acc_sc[...] += _dot(a.astype(BF16), wd_ref[...])

    @pl.when(j == pl.num_programs(1) - 1)
    def _():
        y = (o_ref[...] if has_mix else x_ref[...]) + 0.5 * acc_sc[...]
        if final:
            y = _rms(y, gf_ref[...])
        o_ref[...] = y


def _ffn(x, g, wg, wu, wd, mix=None, gf=None, tm=512, tf=512):
    n = x.shape[0]
    tm = min(tm, n)
    grid = (n // tm, D_FF // tf)
    row = lambda i, j: (i, 0)
    in_specs = [pl.BlockSpec((tm, D_MODEL), row), _const_spec((1, D_MODEL)),
                pl.BlockSpec((D_MODEL, tf), lambda i, j: (0, j)), pl.BlockSpec((D_MODEL, tf), lambda i, j: (0, j)),
                pl.BlockSpec((tf, D_MODEL), lambda i, j: (j, 0))]
    args = [x, g, wg, wu, wd]
    if mix is not None:
        o_f, o_n, w_out = mix
        in_specs += [pl.BlockSpec((tm, D_FOX), row), pl.BlockSpec((tm, D_NSA), row),
                     _const_spec((D_FOX + D_NSA, D_MODEL))]
        args += [o_f, o_n, w_out]
    if gf is not None:
        in_specs.append(_const_spec((1, D_MODEL)))
        args.append(gf)
    return pl.pallas_call(
        functools.partial(_ffn_kernel, has_mix=mix is not None, final=gf is not None),
        out_shape=jax.ShapeDtypeStruct((n, D_MODEL), F32),
        grid=grid, in_specs=in_specs, out_specs=pl.BlockSpec((tm, D_MODEL), row),
        scratch_shapes=[pltpu.VMEM((tm, D_MODEL), BF16), pltpu.VMEM((tm, D_MODEL), F32)],
        compiler_params=_cparams(("parallel", "arbitrary")), name="ffn")(*args)


def _proj_kernel(*refs, with_cmp):
    x_ref, g_ref, wa_ref, wb_ref, b_ref, bt_ref, qf_ref, qn_ref, misc_ref, fkvt_ref, nkvt_ref, wint_ref, misct_ref = (
        refs[:13])
    tm = x_ref.shape[0]
    h = _rms(x_ref[...], g_ref[...]).astype(BF16)
    qf_ref[...] = (_dot(h, wa_ref[:, _A_QF:_A_QN]) * SCALE).astype(BF16)
    qn_ref[...] = (_dot(h, wa_ref[:, _A_QN:_A_KCVC]) * SCALE).astype(BF16)
    z = _dot(h, wa_ref[:, _A_MISC:_A_END]) + b_ref[...]
    lane = lax.broadcasted_iota(jnp.int32, z.shape, 1)
    misc_ref[...] = jnp.where(lane < N_FOX, _log_sigmoid(z), _sigmoid(z))
    fkvt_ref[...] = _dot_nt(wb_ref[_B_FKV:_B_NKV, :], h)
    nkvt_ref[...] = _dot_nt(wb_ref[_B_NKV:_B_WIN, :], h)
    wint_ref[...] = _dot_nt(wb_ref[_B_WIN:_B_MISC, :], h)
    zt = _dot_nt(wb_ref[_B_MISC:_B_END, :], h) + bt_ref[...]
    sub = lax.broadcasted_iota(jnp.int32, zt.shape, 0)
    misct_ref[...] = jnp.where(sub < N_FOX, _log_sigmoid(zt), _sigmoid(zt))
    if with_cmp:
        xcmp_ref, tr_sc = refs[13:]
        kcvc = _dot(h, wa_ref[:, _A_KCVC:_A_MISC])
        nch = 2 * D_KV // LANES
        for c in range(nch):
            tr_sc[c] = kcvc[:, c * LANES:(c + 1) * LANES]
        for l in range(STRIDE):
            for c in range(nch):
                rows = tr_sc[c, pl.ds(l, tm // STRIDE, stride=STRIDE), :]
                for i in range(2):
                    xcmp_ref[2 * c + i, :, l * HEAD_DIM:(l + 1) * HEAD_DIM] = (
                        rows[:, i * HEAD_DIM:(i + 1) * HEAD_DIM])


def _proj(x, g, wa, wb, b, bt, with_cmp, tm=256):
    bsz, t, _ = x.shape
    tm = min(tm, t)
    nt = t // tm
    row = lambda bi, i: (bi, i, 0)
    col = lambda bi, i: (bi, 0, i)
    out_shape = [jax.ShapeDtypeStruct((bsz, t, D_FOX), BF16), jax.ShapeDtypeStruct((bsz, t, D_NSA), BF16),
                 jax.ShapeDtypeStruct((bsz, t, MISC_W), F32),
                 jax.ShapeDtypeStruct((bsz, 2 * D_FOX, t), F32), jax.ShapeDtypeStruct((bsz, 4 * D_KV, t), F32),
                 jax.ShapeDtypeStruct((bsz, 2 * D_KV, t), F32), jax.ShapeDtypeStruct((bsz, MISC_W, t), F32)]
    out_specs = [pl.BlockSpec((None, tm, D_FOX), row), pl.BlockSpec((None, tm, D_NSA), row),
                 pl.BlockSpec((None, tm, MISC_W), row),
                 pl.BlockSpec((None, 2 * D_FOX, tm), col), pl.BlockSpec((None, 4 * D_KV, tm), col),
                 pl.BlockSpec((None, 2 * D_KV, tm), col), pl.BlockSpec((None, MISC_W, tm), col)]
    scratch = []
    if with_cmp:
        out_shape.append(jax.ShapeDtypeStruct((2 * N_KV, bsz * t // STRIDE, CMP_ROW), F32))
        out_specs.append(pl.BlockSpec((2 * N_KV, tm // STRIDE, CMP_ROW), lambda bi, i: (0, bi * nt + i, 0)))
        scratch.append(pltpu.VMEM((2 * D_KV // LANES, tm, LANES), F32))
    return pl.pallas_call(
        functools.partial(_proj_kernel, with_cmp=with_cmp), out_shape=tuple(out_shape), grid=(bsz, nt),
        in_specs=[pl.BlockSpec((None, tm, D_MODEL), row), _const_spec((1, D_MODEL)),
                  _const_spec((D_MODEL, _A_END)), _const_spec((_B_END, D_MODEL)),
                  _const_spec((1, MISC_W)), _const_spec((MISC_W, 1))],
        out_specs=tuple(out_specs), scratch_shapes=scratch,
        compiler_params=_cparams(("parallel", "parallel")), name="proj")(x, g, wa, wb, b, bt)


def _cumsum_kernel(lf_ref, tri_ref, c_ref):
    c_ref[...] = _dot3(lf_ref[...], tri_ref[...])


def _cumsum(misct):
    b, _, t = misct.shape
    tri = jnp.triu(jnp.ones((t, t), BF16))
    return pl.pallas_call(
        _cumsum_kernel, out_shape=jax.ShapeDtypeStruct((b, N_FOX, t), F32), grid=(b,),
        in_specs=[pl.BlockSpec((None, N_FOX, t), lambda i: (i, 0, 0)), _const_spec((t, t))],
        out_specs=pl.BlockSpec((None, N_FOX, t), lambda i: (i, 0, 0)),
        compiler_params=_cparams(("parallel",)), name="fox_cumsum")(misct, tri)


def _fox_kernel(q_ref, k_ref, v_ref, c_ref, o_ref, kb_sc, vb_sc, cb_sc, *, tq, nh):
    qi = pl.program_id(2)
    t = k_ref.shape[1]
    tk = tq
    nk = t // tk

    @pl.when(qi == 0)
    def _():
        for j in range(nk):
            cols = slice(j * tk, (j + 1) * tk)
            cb_sc[j] = c_ref[:, cols]
            for h in range(nh):
                rows = slice(h * HEAD_DIM, (h + 1) * HEAD_DIM)
                kb_sc[j, h] = k_ref[rows, cols].astype(BF16)
                vb_sc[j, h] = v_ref[rows, cols].astype(BF16)

    q = q_ref[...]
    qs = [q[:, h * HEAD_DIM:(h + 1) * HEAD_DIM] for h in range(nh)]
    causal = (lax.broadcasted_iota(jnp.int32, (tq, tk), 1) <= lax.broadcasted_iota(jnp.int32, (tq, tk), 0))

    def step(kj, carry, diagonal):
        ck = cb_sc[kj]
        out = []
        for h in range(nh):
            m, l, acc = carry[h]
            s = _dot(qs[h], kb_sc[kj, h]) - ck[h:h + 1, :]
            if diagonal:
                s = jnp.where(causal, s, NEG)
            m, alpha, l, p = _online_softmax_step(s, m, l)
            acc = alpha * acc + _dot_nt(p.astype(BF16), vb_sc[kj, h])
            out.append((m, l, acc))
        return tuple(out)

    init = tuple((jnp.full((tq, 1), NEG, F32), jnp.zeros((tq, 1), F32), jnp.zeros((tq, HEAD_DIM), F32))
                 for _ in range(nh))
    carry = lax.fori_loop(0, qi, functools.partial(step, diagonal=False), init)
    carry = step(qi, carry, True)
    o_ref[...] = jnp.concatenate([acc / jnp.maximum(l, 1e-30) for _, l, acc in carry], axis=1).astype(BF16)


def _fox_prompt(qf, fkvt, crow, tq=256, nh=4):
    b, t, _ = qf.shape
    tq = min(tq, t)
    nk = t // tq
    ng = N_FOX // nh
    w = nh * HEAD_DIM
    crow = crow.reshape(b, ng, nh, t)
    return pl.pallas_call(
        functools.partial(_fox_kernel, tq=tq, nh=nh),
        out_shape=jax.ShapeDtypeStruct((b, t, D_FOX), BF16),
        grid=(b, ng, t // tq),
        in_specs=[pl.BlockSpec((None, tq, w), lambda bi, hg, qi: (bi, qi, hg)),
                  pl.BlockSpec((None, w, t), lambda bi, hg, qi: (bi, hg, 0)),
                  pl.BlockSpec((None, w, t), lambda bi, hg, qi: (bi, ng + hg, 0)),
                  pl.BlockSpec((None, None, nh, t), lambda bi, hg, qi: (bi, hg, 0, 0))],
        out_specs=pl.BlockSpec((None, tq, w), lambda bi, hg, qi: (bi, qi, hg)),
        scratch_shapes=[pltpu.VMEM((nk, nh, HEAD_DIM, tq), BF16), pltpu.VMEM((nk, nh, HEAD_DIM, tq), BF16),
                        pltpu.VMEM((nk, nh, tq), F32)],
        compiler_params=_cparams(("parallel", "parallel", "arbitrary")), name="fox_prompt")(qf, fkvt, fkvt, crow)


def _compress_rows(x, pe_ref, w1_ref, w2_ref):
    r = x.shape[0]
    xa = (x + pe_ref[0:1, :]).astype(BF16)
    xb = (x + pe_ref[1:2, :]).astype(BF16)
    ha = _dot(xa, w1_ref[:CMP_ROW, :])
    hb = _dot(xb, w1_ref[CMP_ROW:, :])
    hid = ha + pltpu.roll(hb, r - 1, 0)
    gl = 0.5 * hid * (1.0 + jnp.tanh(0.7978845608028654 * (hid + 0.044715 * (hid * hid * hid))))
    return _dot(gl.astype(BF16), w2_ref[...])


def _compress_kernel(x_ref, pe_ref, w1_ref, w2_ref, o_ref):
    o_ref[...] = _compress_rows(x_ref[...], pe_ref, w1_ref, w2_ref).astype(BF16)


def _compress_prompt(xcmp, pe2, w1, w2, b, t):
    r = t // STRIDE
    return pl.pallas_call(
        _compress_kernel,
        out_shape=jax.ShapeDtypeStruct((b, 2 * N_KV, r, HEAD_DIM), BF16),
        grid=(b, 2 * N_KV),
        in_specs=[pl.BlockSpec((None, r, CMP_ROW), lambda bi, j: (j, bi, 0)),
                  pl.BlockSpec((None, 2, CMP_ROW), lambda bi, j: (j // N_KV, 0, 0)),
                  pl.BlockSpec((None, L_CMP * HEAD_DIM, LANES), lambda bi, j: (j // N_KV, 0, 0)),
                  pl.BlockSpec((None, LANES, HEAD_DIM), lambda bi, j: (j // N_KV, 0, 0))],
        out_specs=pl.BlockSpec((None, None, r, HEAD_DIM), lambda bi, j: (bi, j, 0, 0)),
        compiler_params=_cparams(("parallel", "parallel")), name="nsa_compress")(xcmp, pe2, w1, w2)


def _slope_col(slopes_ref, g, hrow):
    s0, s1, s2, s3 = (slopes_ref[g, h] for h in range(HPG))
    return jnp.where(hrow == 0, s0, jnp.where(hrow == 1, s1, jnp.where(hrow == 2, s2, s3)))


def _select_blocks(p_slc, cur, nsb, axis):
    jb = lax.broadcasted_iota(jnp.int32, p_slc.shape, axis)
    valid = jb <= cur
    forced = valid & ((jb == 0) | ((cur - jb) < N_LOCAL))
    score = jnp.where(valid, p_slc, -jnp.inf)
    score = jnp.where(forced, jnp.inf, score)
    rank = jnp.zeros(p_slc.shape, jnp.int32)
    for i in range(nsb):
        ci = score[:, i:i + 1] if axis == 1 else score[i:i + 1, :]
        beats = (ci > score) | ((ci == score) & (jb > i))
        rank = rank + jnp.where(beats, 1, 0)
    return jnp.where((rank < N_SEL) & (jb < nsb), 1.0, 0.0)


def _cmp_branch(q4, kc, vc, slope, qpos, tq):
    ncp = kc.shape[0]
    s = _dot_nt(q4, kc)
    cend = lax.broadcasted_iota(jnp.int32, (1, ncp), 1) * STRIDE + (L_CMP - 1)
    s = s - slope * (qpos.astype(F32) - cend.astype(F32))
    pc = _masked_softmax(s, cend <= qpos)
    o_c = _dot(pc.astype(BF16), vc)
    return o_c, pc[0:tq] + pc[tq:2 * tq] + pc[2 * tq:3 * tq] + pc[3 * tq:4 * tq]


def _gate_col(misc, br, g):
    cols = [_lane_pick(misc, GATE_OFF + br * N_NSA + g * HPG + h) for h in range(HPG)]
    return jnp.concatenate(cols, axis=0)


def _slc_matrix(nc, nsb, ncp, width):
    r, c = L_SEL // STRIDE, L_CMP // STRIDE
    offs, counts = np.unique((np.arange(r)[:, None] - np.arange(c)[None, :]).ravel(), return_counts=True)
    m = np.zeros((ncp, width), np.float32)
    for j in range(nsb):
        for off, cnt in zip(offs, counts):
            n = r * j + off
            if 0 <= n < nc:
                m[n, j] += cnt
    return jnp.asarray(m, BF16)


def _alibi_slopes():
    base = 2.0 ** (-8.0 * np.arange(1, N_NSA + 1, dtype=np.float32) / N_NSA)
    return jnp.asarray(base.astype(np.float32).reshape(HPG, N_KV).T)


def _nsa_kernel(slopes_ref, q_ref, misc_ref, cmp_ref, ks_ref, vs_ref, kw_ref, vw_ref, m_ref, e_ref, o_ref,
                ksb, vsb, kwb, vwb, *, tq, tk, nsb):
    g = pl.program_id(1)
    qi = pl.program_id(2)
    t = ks_ref.shape[1]
    nk = t // tk
    wch = LANES
    nwc = t // wch

    @pl.when(qi == 0)
    def _():
        for j in range(nk):
            ksb[j] = ks_ref[:, j * tk:(j + 1) * tk].astype(BF16)
            vsb[j] = vs_ref[:, j * tk:(j + 1) * tk].astype(BF16)
        for j in range(nwc):
            kwb[j] = kw_ref[:, j * wch:(j + 1) * wch].astype(BF16)
            vwb[j] = vw_ref[:, j * wch:(j + 1) * wch].astype(BF16)

    a = qi * tq
    rows = HPG * tq
    row = lax.broadcasted_iota(jnp.int32, (rows, 1), 0)
    qpos = a + row % tq
    hrow = row // tq
    qpos1 = a + lax.broadcasted_iota(jnp.int32, (tq, 1), 0)
    misc = misc_ref[...]
    slope = _slope_col(slopes_ref, g, hrow)
    q4 = jnp.concatenate([q_ref[:, h * HEAD_DIM:(h + 1) * HEAD_DIM] for h in range(HPG)], axis=0)

    o_c, pg = _cmp_branch(q4, cmp_ref[g], cmp_ref[N_KV + g], slope, qpos, tq)
    hi, mid, lo = _split3(pg)
    mt = m_ref[...]
    p_slct = _dot_nt(mt, hi) + _dot_nt(mt, mid) + _dot_nt(mt, lo)
    cur = (a + lax.broadcasted_iota(jnp.int32, (1, tq), 1)) // L_SEL
    selt = _select_blocks(p_slct, cur, nsb, 0)
    sel = jnp.concatenate([selt, jnp.zeros((LANES - selt.shape[0], tq), F32)], axis=0).T.astype(BF16)

    def step(kj, carry, diagonal):
        kpos = kj * tk + lax.broadcasted_iota(jnp.int32, (1, tk), 1)
        keep = _dot(sel, e_ref[kj]) > 0.5
        if diagonal:
            keep = keep & (kpos <= qpos1)
        s4 = _dot(q4, ksb[kj])
        drel = (kpos - a).astype(F32)
        out = []
        for h in range(HPG):
            m, l, acc = carry[h]
            s = s4[h * tq:(h + 1) * tq] + slopes_ref[g, h] * drel
            s = jnp.where(keep, s, NEG)
            m, alpha, l, p = _online_softmax_step(s, m, l)
            acc = alpha * acc + _dot_nt(p.astype(BF16), vsb[kj])
            out.append((m, l, acc))
        return tuple(out)

    init = tuple((jnp.full((tq, 1), NEG, F32), jnp.zeros((tq, 1), F32), jnp.zeros((tq, HEAD_DIM), F32))
                 for _ in range(HPG))
    kd = a // tk
    carry = lax.fori_loop(0, kd, functools.partial(step, diagonal=False), init)
    carry = step(kd, carry, True)
    o_s = jnp.concatenate([acc / jnp.maximum(l, 1e-30) for _, l, acc in carry], axis=0)

    nwin = min(WINDOW // wch + tq // wch, nwc)
    c0 = a // wch + tq // wch - nwin
    s_parts, kpos_parts, cis = [], [], []
    for i in range(nwin):
        ci = jnp.maximum(c0 + i, 0)
        kp = ci * wch + lax.broadcasted_iota(jnp.int32, (1, wch), 1)
        kpos_parts.append(jnp.where(c0 + i >= 0, kp, -2 * WINDOW - t))
        s_parts.append(_dot(q4, kwb[ci]))
        cis.append(ci)
    dist = qpos - jnp.concatenate(kpos_parts, axis=1)
    s = jnp.concatenate(s_parts, axis=1) - slope * dist.astype(F32)
    pw = _masked_softmax(s, (dist >= 0) & (dist < WINDOW)).astype(BF16)
    o_w = _dot_nt(pw[:, 0:wch], vwb[cis[0]])
    for i in range(1, nwin):
        o_w = o_w + _dot_nt(pw[:, i * wch:(i + 1) * wch], vwb[cis[i]])

    o = _gate_col(misc, 0, g) * o_c + _gate_col(misc, 1, g) * o_s + _gate_col(misc, 2, g) * o_w
    o_ref[...] = jnp.concatenate([o[h * tq:(h + 1) * tq] for h in range(HPG)], axis=1).astype(BF16)


def _nsa_prompt(qn, misc, cmp, nkvt, wint, tq=128, tk=256):
    b, t, _ = qn.shape
    tq = min(tq, t)
    tk = min(tk, t)
    assert tk % tq == 0 and tq % LANES == 0
    ncp = t // STRIDE
    nc = (t - L_CMP) // STRIDE + 1
    nsb = -(-t // L_SEL)
    assert nsb <= LANES
    nsbp = -(-nsb // SUBLANES) * SUBLANES
    m = _slc_matrix(nc, nsb, ncp, nsbp).T
    e = (np.arange(t)[None, :] // L_SEL == np.arange(LANES)[:, None]).astype(np.float32)
    e = jnp.asarray(e.reshape(LANES, t // tk, tk).transpose(1, 0, 2), BF16)
    hw = HPG * HEAD_DIM
    grid_spec = pl.GridSpec(
        grid=(b, N_KV, t // tq),
        in_specs=[pl.BlockSpec(memory_space=pltpu.SMEM),
                  pl.BlockSpec((None, tq, hw), lambda bi, g, qi: (bi, qi, g)),
                  pl.BlockSpec((None, tq, MISC_W), lambda bi, g, qi: (bi, qi, 0)),
                  pl.BlockSpec((None, 2 * N_KV, ncp, HEAD_DIM), lambda bi, g, qi: (bi, 0, 0, 0)),
                  pl.BlockSpec((None, HEAD_DIM, t), lambda bi, g, qi: (bi, 2 * N_KV + g, 0)),
                  pl.BlockSpec((None, HEAD_DIM, t), lambda bi, g, qi: (bi, 3 * N_KV + g, 0)),
                  pl.BlockSpec((None, HEAD_DIM, t), lambda bi, g, qi: (bi, g, 0)),
                  pl.BlockSpec((None, HEAD_DIM, t), lambda bi, g, qi: (bi, N_KV + g, 0)),
                  _const_spec((nsbp, ncp)), _const_spec((t // tk, LANES, tk))],
        out_specs=pl.BlockSpec((None, tq, hw), lambda bi, g, qi: (bi, qi, g)),
        scratch_shapes=[pltpu.VMEM((t // tk, HEAD_DIM, tk), BF16), pltpu.VMEM((t // tk, HEAD_DIM, tk), BF16),
                        pltpu.VMEM((t // LANES, HEAD_DIM, LANES), BF16),
                        pltpu.VMEM((t // LANES, HEAD_DIM, LANES), BF16)])
    return pl.pallas_call(
        functools.partial(_nsa_kernel, tq=tq, tk=tk, nsb=nsb),
        out_shape=jax.ShapeDtypeStruct((b, t, D_NSA), BF16), grid_spec=grid_spec,
        compiler_params=_cparams(("parallel", "parallel", "arbitrary")), name="nsa_prompt",
    )(_alibi_slopes(), qn, misc, cmp, nkvt, nkvt, wint, wint, m, e)


def _fox_dec_kernel(*refs, n_new, ppb):
    pt_ref, q_ref = refs[:2]
    kv_refs = refs[2:2 + ppb]
    lf_refs = refs[2 + ppb:2 + 2 * ppb]
    kvn_ref, lfn_ref, gt_ref, le_ref, o_ref, qbd_sc, m_sc, l_sc, r_sc, acc_sc = refs[2 + 2 * ppb:]
    p = pl.program_id(1)
    page = kvn_ref.shape[2]
    rows = n_new * N_FOX

    def tile_heads(x):
        return jnp.concatenate([x] * n_new, axis=0)

    def accumulate(kt, vt, bias, mask):
        s = _dot(qbd_sc[...], kt) + bias
        if mask is not None:
            s = jnp.where(mask, s, NEG)
        m, alpha, l, pr = _online_softmax_step(s, m_sc[...], l_sc[...])
        m_sc[...] = m
        l_sc[...] = l
        acc_sc[...] = alpha * acc_sc[...] + _dot_nt(pr.astype(BF16), vt)

    @pl.when(p == 0)
    def _():
        q = q_ref[...]
        rep = jnp.concatenate([jnp.broadcast_to(q[t:t + 1, :], (N_FOX, D_FOX)) for t in range(n_new)], axis=0)
        rr = lax.broadcasted_iota(jnp.int32, (rows, D_FOX), 0)
        ll = lax.broadcasted_iota(jnp.int32, (rows, D_FOX), 1)
        qbd_sc[...] = jnp.where(ll // HEAD_DIM == rr % N_FOX, rep, 0.0).astype(BF16)
        m_sc[...] = jnp.full(m_sc.shape, NEG, F32)
        l_sc[...] = jnp.zeros(l_sc.shape, F32)
        r_sc[...] = jnp.zeros(r_sc.shape, F32)
        acc_sc[...] = jnp.zeros(acc_sc.shape, F32)
        cn = _dot3(lfn_ref[...], le_ref[...])
        tkey = lax.broadcasted_iota(jnp.int32, (rows, page), 1)
        tqry = lax.broadcasted_iota(jnp.int32, (rows, page), 0) // N_FOX
        accumulate(kvn_ref[0].astype(BF16), kvn_ref[1].astype(BF16), -tile_heads(cn),
                   (tkey <= tqry) & (tkey < n_new))

    later = r_sc[...]
    suffix = []
    for lf_ref in lf_refs:
        lf = lf_ref[...]
        suffix.append(_dot3(lf, gt_ref[...]) + later)
        later = later + jnp.sum(lf, axis=1, keepdims=True)
    r_sc[...] = later
    accumulate(jnp.concatenate([r[0].astype(BF16) for r in kv_refs], axis=1),
               jnp.concatenate([r[1].astype(BF16) for r in kv_refs], axis=1),
               tile_heads(jnp.concatenate(suffix, axis=1)), None)

    @pl.when(p == pl.num_programs(1) - 1)
    def _():
        o = acc_sc[...] / jnp.maximum(l_sc[...], 1e-30)
        rr = lax.broadcasted_iota(jnp.int32, o.shape, 0)
        ll = lax.broadcasted_iota(jnp.int32, o.shape, 1)
        o = jnp.where(ll // HEAD_DIM == rr % N_FOX, o, 0.0)
        orow = lax.broadcasted_iota(jnp.int32, (SQ, D_FOX), 0)
        out = jnp.zeros((SQ, D_FOX), F32)
        for t in range(n_new):
            out = jnp.where(orow == t, jnp.sum(o[t * N_FOX:(t + 1) * N_FOX], axis=0, keepdims=True), out)
        o_ref[...] = out


def _fox_decode(pt_flat, qf_pad, cache_kvt, cache_lft, kvt_new, lft_new, n_pages, n_new):
    db = qf_pad.shape[0]
    page = cache_kvt.shape[3]
    gt = np.tril(np.ones((page, page), np.float32), -1)
    le = np.triu(np.ones((page, page), np.float32))
    ppb = next(c for c in (4, 2, 1) if n_pages % c == 0)
    pidx = lambda i: (lambda b, p, pt: (pt[b * n_pages + n_pages - 1 - (p * ppb + i)], 0, 0, 0))
    pidx3 = lambda i: (lambda b, p, pt: (pt[b * n_pages + n_pages - 1 - (p * ppb + i)], 0, 0))
    rows = n_new * N_FOX
    grid_spec = pltpu.PrefetchScalarGridSpec(
        num_scalar_prefetch=1, grid=(db, n_pages // ppb),
        in_specs=[pl.BlockSpec((None, SQ, D_FOX), lambda b, p, pt: (b, 0, 0))]
        + [pl.BlockSpec((None, 2, D_FOX, page), pidx(i)) for i in range(ppb)]
        + [pl.BlockSpec((None, N_FOX, page), pidx3(i)) for i in range(ppb)]
        + [pl.BlockSpec((None, 2, D_FOX, page), lambda b, p, pt: (b, 0, 0, 0)),
           pl.BlockSpec((None, N_FOX, page), lambda b, p, pt: (b, 0, 0)),
           _const_spec((page, page)), _const_spec((page, page))],
        out_specs=pl.BlockSpec((None, SQ, D_FOX), lambda b, p, pt: (b, 0, 0)),
        scratch_shapes=[pltpu.VMEM((rows, D_FOX), BF16), pltpu.VMEM((rows, 1), F32), pltpu.VMEM((rows, 1), F32),
                        pltpu.VMEM((N_FOX, 1), F32), pltpu.VMEM((rows, D_FOX), F32)])
    return pl.pallas_call(
        functools.partial(_fox_dec_kernel, n_new=n_new, ppb=ppb),
        out_shape=jax.ShapeDtypeStruct((db, SQ, D_FOX), F32), grid_spec=grid_spec,
        compiler_params=_cparams(("parallel", "arbitrary")), name="fox_decode",
    )(pt_flat, qf_pad, *([cache_kvt] * ppb), *([cache_lft] * ppb), kvt_new, lft_new,
      jnp.asarray(gt, BF16), jnp.asarray(le, BF16))


def _nsa_pages_kernel(pt_ref, pg_ref, new_ref, pe_ref, w1_ref, w2_ref, ksvs_ref, cmp_ref, tr_sc, x_sc):
    p = pl.program_id(1)
    npg = pl.num_programs(1) - 1
    page = pg_ref.shape[2]
    rpp = page // STRIDE
    nch = D_KV // LANES

    @pl.when(p < npg)
    def _():
        ksvs_ref[...] = jnp.concatenate([pg_ref[2], pg_ref[3]], axis=0).astype(BF16)
        for s in range(2):
            for c in range(nch):
                tr_sc[s * nch + c] = pg_ref[s, c * LANES:(c + 1) * LANES, :].T
        r0 = pl.multiple_of(p * rpp, rpp)
        for l in range(STRIDE):
            for c in range(2 * nch):
                rows = tr_sc[c, pl.ds(l, rpp, stride=STRIDE), :]
                for i in range(2):
                    x_sc[2 * c + i, pl.ds(r0, rpp), l * HEAD_DIM:(l + 1) * HEAD_DIM] = (
                        rows[:, i * HEAD_DIM:(i + 1) * HEAD_DIM])

    @pl.when(p == npg)
    def _():
        ksvs_ref[...] = new_ref[...].astype(BF16)
        for j in range(2 * N_KV):
            s = j // N_KV
            cmp_ref[j] = _compress_rows(x_sc[j], pe_ref.at[s], w1_ref.at[s], w2_ref.at[s]).astype(BF16)


def _nsa_pages(pt_flat, cache_nsat, ksvst_new, pe2, w1, w2, n_pages):
    db = ksvst_new.shape[0]
    page = cache_nsat.shape[3]
    past = n_pages * page
    rows = past // STRIDE
    pidx = lambda b, p, pt: (pt[b * n_pages + jnp.minimum(p, n_pages - 1)], 0, 0, 0)
    grid_spec = pltpu.PrefetchScalarGridSpec(
        num_scalar_prefetch=1, grid=(db, n_pages + 1),
        in_specs=[pl.BlockSpec((None, 4, D_KV, page), pidx),
                  pl.BlockSpec((None, 2 * D_KV, page), lambda b, p, pt: (b, 0, 0)),
                  _const_spec((2, 2, CMP_ROW)), _const_spec((2, L_CMP * HEAD_DIM, LANES)),
                  _const_spec((2, LANES, HEAD_DIM))],
        out_specs=(pl.BlockSpec((None, 2 * D_KV, page), lambda b, p, pt: (b, 0, p)),
                   pl.BlockSpec((None, 2 * N_KV, rows, HEAD_DIM), lambda b, p, pt: (b, 0, 0, 0))),
        scratch_shapes=[pltpu.VMEM((2 * D_KV // LANES, page, LANES), F32),
                        pltpu.VMEM((2 * N_KV, rows, CMP_ROW), F32)])
    return pl.pallas_call(
        _nsa_pages_kernel,
        out_shape=(jax.ShapeDtypeStruct((db, 2 * D_KV, past + page), BF16),
                   jax.ShapeDtypeStruct((db, 2 * N_KV, rows, HEAD_DIM), BF16)),
        grid_spec=grid_spec, compiler_params=_cparams(("parallel", "arbitrary")), name="nsa_pages",
    )(pt_flat, cache_nsat, ksvst_new, pe2, w1, w2)


def _nsa_dec_kernel(slopes_ref, q_ref, misc_ref, cmp_ref, ksvs_ref, win_ref, winn_ref, m_ref, e_ref, o_ref,
                    selx_sc, *, past, nsb):
    tq = SQ
    rows = HPG * tq
    row = lax.broadcasted_iota(jnp.int32, (rows, 1), 0)
    qpos = past + row % tq
    qposf = qpos.astype(F32)
    hrow = row // tq
    misc = misc_ref[...]
    q = q_ref[...]
    cur = (past + lax.broadcasted_iota(jnp.int32, (tq, 1), 0)) // L_SEL
    nkeys = ksvs_ref.shape[1]

    o_cs, sels, q4s, slopes = [], [], [], []
    for g in range(N_KV):
        slope = _slope_col(slopes_ref, g, hrow)
        q4 = jnp.concatenate(
            [q[:, (g * HPG + h) * HEAD_DIM:(g * HPG + h + 1) * HEAD_DIM] for h in range(HPG)], axis=0).astype(BF16)
        o_c, pg = _cmp_branch(q4, cmp_ref[g], cmp_ref[N_KV + g], slope, qpos, tq)
        sels.append(_select_blocks(_dot3(pg, m_ref[...]), cur, nsb, 1))
        o_cs.append(o_c)
        q4s.append(q4)
        slopes.append(slope)
    selx_sc[...] = _dot(jnp.concatenate(sels, axis=0).astype(BF16), e_ref[...])

    kpos = lax.broadcasted_iota(jnp.int32, (1, nkeys), 1)
    nwin = win_ref.shape[2]
    wpos = (past - nwin) + lax.broadcasted_iota(jnp.int32, (1, nwin + winn_ref.shape[2]), 1)
    outs = []
    for g in range(N_KV):
        q4, slope = q4s[g], slopes[g]
        gr = slice(g * HEAD_DIM, (g + 1) * HEAD_DIM)
        s = _dot(q4, ksvs_ref[gr, :]) - slope * (qposf - kpos.astype(F32))
        selx = selx_sc[g * tq:(g + 1) * tq, :]
        mask = (jnp.concatenate([selx] * HPG, axis=0) > 0.5) & (kpos <= qpos)
        o_s = _dot_nt(_masked_softmax(s, mask).astype(BF16), ksvs_ref[D_KV + g * HEAD_DIM:D_KV + (g + 1) * HEAD_DIM, :])
        kw = jnp.concatenate([win_ref[0, gr, :], winn_ref[0, gr, :]], axis=1).astype(BF16)
        vw = jnp.concatenate([win_ref[1, gr, :], winn_ref[1, gr, :]], axis=1).astype(BF16)
        dist = qpos - wpos
        s = _dot(q4, kw) - slope * dist.astype(F32)
        o_w = _dot_nt(_masked_softmax(s, (dist >= 0) & (dist < WINDOW)).astype(BF16), vw)
        o = _gate_col(misc, 0, g) * o_cs[g] + _gate_col(misc, 1, g) * o_s + _gate_col(misc, 2, g) * o_w
        outs += [o[h * tq:(h + 1) * tq] for h in range(HPG)]
    o_ref[...] = jnp.concatenate(outs, axis=1)


def _nsa_decode(qn_pad, misc_pad, cmp, ksvst, wint_state, wint_new, past, n_new):
    db = qn_pad.shape[0]
    nkeys = ksvst.shape[2]
    ncp = cmp.shape[2]
    tk_all = past + n_new
    nc = (tk_all - L_CMP) // STRIDE + 1
    nsb = -(-tk_all // L_SEL)
    width = -(-nsb // LANES) * LANES
    m = _slc_matrix(nc, nsb, ncp, width)
    e = jnp.asarray(np.arange(nkeys)[None, :] // L_SEL == np.arange(width)[:, None], BF16)
    nwin = wint_state.shape[3]
    nwn = wint_new.shape[3]
    per_b3 = lambda b: (b, 0, 0)
    per_b4 = lambda b: (b, 0, 0, 0)
    grid_spec = pl.GridSpec(
        grid=(db,),
        in_specs=[pl.BlockSpec(memory_space=pltpu.SMEM),
                  pl.BlockSpec((None, SQ, D_NSA), per_b3), pl.BlockSpec((None, SQ, MISC_W), per_b3),
                  pl.BlockSpec((None, 2 * N_KV, ncp, HEAD_DIM), per_b4),
                  pl.BlockSpec((None, 2 * D_KV, nkeys), per_b3),
                  pl.BlockSpec((None, 2, D_KV, nwin), per_b4), pl.BlockSpec((None, 2, D_KV, nwn), per_b4),
                  _const_spec((ncp, width)), _const_spec((width, nkeys))],
        out_specs=pl.BlockSpec((None, SQ, D_NSA), per_b3),
        scratch_shapes=[pltpu.VMEM((N_KV * SQ, nkeys), F32)])
    return pl.pallas_call(
        functools.partial(_nsa_dec_kernel, past=past, nsb=nsb),
        out_shape=jax.ShapeDtypeStruct((db, SQ, D_NSA), F32), grid_spec=grid_spec,
        compiler_params=_cparams(("parallel",)), name="nsa_decode",
    )(_alibi_slopes(), qn_pad, misc_pad, cmp, ksvst, wint_state, wint_new, m, e)


def _prep_w_in(w_in, b_forget, b_gate):
    sizes = (D_FOX, D_FOX, D_FOX, N_FOX, D_NSA) + (D_KV,) * 6 + (3 * N_NSA,)
    offs = np.cumsum((0,) + sizes)
    part = lambda i, j=None: w_in[:, offs[i]:offs[(i if j is None else j) + 1]]
    pad = MISC_W - N_FOX - 3 * N_NSA
    misc = jnp.concatenate([part(3), part(11), jnp.zeros((w_in.shape[0], pad), w_in.dtype)], axis=1)
    wa = jnp.concatenate([part(0), part(4), part(5, 6), misc], axis=1).astype(BF16)
    wb = jnp.concatenate([part(1, 2), part(5, 10), misc], axis=1).T.astype(BF16)
    b = jnp.concatenate([b_forget, b_gate, jnp.zeros((pad,), F32)]).astype(F32)
    return wa, wb, b.reshape(1, MISC_W), b.reshape(MISC_W, 1)


def _pos_minor(x):
    nd = x.ndim
    return jnp.transpose(x, tuple(range(nd - 4)) + (nd - 3, nd - 2, nd - 1, nd - 4))


def _pos_major(x):
    nd = x.ndim
    return jnp.transpose(x, tuple(range(nd - 4)) + (nd - 1, nd - 4, nd - 3, nd - 2))


def kernel(x_prompt, x_sample, cache_fox_kv, cache_fox_logf, cache_nsa_kv, state_win_kv, page_table, norm_ffn1,
           ffn1_gate, ffn1_up, ffn1_down, norm_mix, w_in, b_forget, b_gate, phi_pe, phi_w1, phi_w2, w_out,
           norm_ffn2, ffn2_gate, ffn2_up, ffn2_down, norm_final):
    depth = norm_ffn1.shape[0]
    bsz, t, _ = x_prompt.shape
    db, s_new, _ = x_sample.shape
    n_pool, page = cache_fox_kv.shape[1], cache_fox_kv.shape[2]
    n_pages = page_table.shape[1]
    past = n_pages * page
    wb_len = state_win_kv.shape[2]
    wp = min(WINDOW, t)
    assert s_new <= SQ and wb_len == min(WINDOW, past)

    xp = x_prompt
    xs = x_sample.reshape(1, db * s_new, D_MODEL)
    fox_kvt_pool = _pos_minor(cache_fox_kv).reshape(depth * n_pool, 2, D_FOX, page)
    fox_lft_pool = jnp.swapaxes(cache_fox_logf, 2, 3).reshape(depth * n_pool, N_FOX, page)
    nsa_kvt_pool = _pos_minor(cache_nsa_kv).reshape(depth * n_pool, 4, D_KV, page)
    win_t = _pos_minor(state_win_kv).reshape(depth, db, 2, D_KV, wb_len)
    gfin = norm_final.reshape(1, D_MODEL)
    outs = [[] for _ in range(8)]

    def pad_rows(a):
        a = a.reshape(db, s_new, -1).astype(F32)
        return jnp.pad(a, ((0, 0), (0, SQ - s_new), (0, 0)))

    def new_cols(a_t, lead):
        a_t = jnp.transpose(a_t.reshape(a_t.shape[1], db, s_new), (1, 0, 2))
        return jnp.pad(a_t, ((0, 0), (0, 0), (0, page - s_new))).reshape((db,) + lead + (page,))

    for l in range(depth):
        last = l == depth - 1
        g1 = norm_ffn1[l].reshape(1, D_MODEL)
        gm = norm_mix[l].reshape(1, D_MODEL)
        g2 = norm_ffn2[l].reshape(1, D_MODEL)
        ffn1 = (ffn1_gate[l].astype(BF16), ffn1_up[l].astype(BF16), ffn1_down[l].astype(BF16))
        ffn2 = (ffn2_gate[l].astype(BF16), ffn2_up[l].astype(BF16), ffn2_down[l].astype(BF16))
        wa, wbt, b_r, b_c = _prep_w_in(w_in[l], b_forget[l], b_gate[l])
        wo = w_out[l].astype(BF16)
        pe2 = phi_pe[l].reshape(2, 2, CMP_ROW)
        w1 = phi_w1[l].astype(BF16)
        w2 = phi_w2[l].astype(BF16)
        pt_flat = (page_table + l * n_pool).reshape(-1).astype(jnp.int32)

        xp = _ffn(xp.reshape(bsz * t, D_MODEL), g1, *ffn1).reshape(bsz, t, D_MODEL)
        xs = _ffn(xs.reshape(db * s_new, D_MODEL), g1, *ffn1).reshape(1, db * s_new, D_MODEL)

        qf, qn, misc, fkvt, nkvt, wint, misct, xcmp = _proj(xp, gm, wa, wbt, b_r, b_c, True)
        o_f = _fox_prompt(qf, fkvt, _cumsum(misct))
        cmp = _compress_prompt(xcmp, pe2, w1, w2, bsz, t)
        o_n = _nsa_prompt(qn, misc, cmp, nkvt, wint)
        xp = _ffn(xp.reshape(bsz * t, D_MODEL), g2, *ffn2,
                  mix=(o_f.reshape(bsz * t, D_FOX), o_n.reshape(bsz * t, D_NSA), wo),
                  gf=gfin if last else None).reshape(bsz, t, D_MODEL)
        outs[0].append(_pos_major(fkvt.reshape(bsz, 2, N_FOX, HEAD_DIM, t)))
        outs[1].append(jnp.swapaxes(misct[:, :N_FOX, :], 1, 2))
        outs[2].append(_pos_major(nkvt.reshape(bsz, 4, N_KV, HEAD_DIM, t)))
        outs[3].append(_pos_major(wint[:, :, t - wp:].reshape(bsz, 2, N_KV, HEAD_DIM, wp)))

        qf, qn, misc, fkvt, nkvt, wint, misct = _proj(xs, gm, wa, wbt, b_r, b_c, False)
        o_f = _fox_decode(pt_flat, pad_rows(qf), fox_kvt_pool, fox_lft_pool, new_cols(fkvt, (2, D_FOX)),
                          new_cols(misct[:, :N_FOX, :], (N_FOX,)), n_pages, s_new)
        ksvst, cmp = _nsa_pages(pt_flat, nsa_kvt_pool, new_cols(nkvt[:, 2 * D_KV:, :], (2 * D_KV,)),
                                pe2, w1, w2, n_pages)
        o_n = _nsa_decode(pad_rows(qn), pad_rows(misc), cmp, ksvst, win_t[l], new_cols(wint, (2, D_KV)),
                          past, s_new)
        o_f = o_f[:, :s_new].reshape(db * s_new, D_FOX).astype(BF16)
        o_n = o_n[:, :s_new].reshape(db * s_new, D_NSA).astype(BF16)
        xs = _ffn(xs.reshape(db * s_new, D_MODEL), g2, *ffn2, mix=(o_f, o_n, wo),
                  gf=gfin if last else None).reshape(1, db * s_new, D_MODEL)
        tok_major = lambda a_t, lead: jnp.transpose(a_t[0], (1, 0)).reshape((db, s_new) + lead)
        win_new = tok_major(wint, (2, N_KV, HEAD_DIM))
        outs[4].append(tok_major(fkvt, (2, N_FOX, HEAD_DIM)))
        outs[5].append(tok_major(misct[:, :N_FOX, :], (N_FOX,)))
        outs[6].append(tok_major(nkvt, (4, N_KV, HEAD_DIM)))
        outs[7].append(jnp.concatenate([state_win_kv[l], win_new], axis=1)[:, -wb_len:])

    y_prompt = xp
    y_sample = xs.reshape(db, s_new, D_MODEL)
    return (y_prompt, y_sample) + tuple(jnp.stack(o) for o in outs)
```

```python
import functools

import numpy as np
import jax
import jax.numpy as jnp
from jax import lax
from jax.experimental import pallas as pl
from jax.experimental.pallas import tpu as pltpu

F32 = jnp.float32
BF16 = jnp.bfloat16

D_MODEL = 2048
HEAD_DIM = 64
N_FOX = 16
N_NSA = 16
N_KV = 4
HPG = N_NSA // N_KV
D_FOX = N_FOX * HEAD_DIM
D_NSA = N_NSA * HEAD_DIM
D_KV = N_KV * HEAD_DIM
D_FF = 5632
L_CMP = 32
STRIDE = 16
L_SEL = 64
N_SEL = 16
N_LOCAL = 2
WINDOW = 512
RMS_EPS = 1e-6
NEG = -1e30
SCALE = HEAD_DIM ** -0.5

LANES = 128
SUBLANES = 8
MISC_W = LANES
GATE_OFF = N_FOX
SQ = SUBLANES
CMP_ROW = STRIDE * HEAD_DIM
V_AUG = HEAD_DIM + 2 * SUBLANES

_A_QF = 0
_A_QN = _A_QF + D_FOX
_A_KCVC = _A_QN + D_NSA
_A_MISC = _A_KCVC + 2 * D_KV
_A_END = _A_MISC + MISC_W
_B_FKV = 0
_B_NKV = _B_FKV + 2 * D_FOX
_B_WIN = _B_NKV + 4 * D_KV
_B_MISC = _B_WIN + 2 * D_KV
_B_END = _B_MISC + MISC_W

_VMEM_LIMIT = 56 * 1024 * 1024


def _cparams(sem):
    return pltpu.CompilerParams(dimension_semantics=sem, vmem_limit_bytes=_VMEM_LIMIT)


def _const_spec(shape):
    nd = len(shape)
    return pl.BlockSpec(shape, lambda *_: (0,) * nd, pipeline_mode=pl.Buffered(1))


def _dot(a, b):
    return jnp.dot(a, b, preferred_element_type=F32)


def _dot_nt(a, b):
    return lax.dot_general(a, b, (((1,), (1,)), ((), ())), preferred_element_type=F32)


def _split3(x):
    hi = x.astype(BF16)
    r1 = x - hi.astype(F32)
    mid = r1.astype(BF16)
    lo = (r1 - mid.astype(F32)).astype(BF16)
    return hi, mid, lo


def _dot3(x, m):
    hi, mid, lo = _split3(x)
    return _dot(hi, m) + _dot(mid, m) + _dot(lo, m)


def _rms(x, g):
    return x * lax.rsqrt(jnp.mean(x * x, axis=-1, keepdims=True) + RMS_EPS) * g


def _masked_softmax(s, mask):
    s = jnp.where(mask, s, NEG)
    e = jnp.where(mask, jnp.exp(s - jnp.max(s, axis=-1, keepdims=True)), 0.0)
    return e / jnp.maximum(jnp.sum(e, axis=-1, keepdims=True), 1e-30)


def _online_softmax_step(s, m, l):
    m_new = jnp.maximum(m, jnp.max(s, axis=1, keepdims=True))
    p = jnp.exp(s - m_new)
    alpha = jnp.exp(m - m_new)
    return m_new, alpha, alpha * l + jnp.sum(p, axis=1, keepdims=True), p


def _aug_values(vt):
    extra = lax.broadcasted_iota(jnp.int32, (V_AUG - HEAD_DIM, vt.shape[1]), 0) == 0
    return jnp.concatenate([vt, jnp.where(extra, 1.0, 0.0)], axis=0).astype(BF16)


def _flash_update(s, m, acc, vt_aug):
    m_new = jnp.maximum(m, jnp.max(s, axis=1, keepdims=True))
    p = jnp.exp(s - m_new)
    return m_new, jnp.exp(m - m_new) * acc + _dot_nt(p.astype(BF16), vt_aug)


def _flash_out(acc):
    return acc[:, :HEAD_DIM] / jnp.maximum(acc[:, HEAD_DIM:HEAD_DIM + 1], 1e-30)


def _lane_pick(x, idx):
    lane = lax.broadcasted_iota(jnp.int32, x.shape, 1)
    return jnp.sum(jnp.where(lane == idx, x, 0.0), axis=1, keepdims=True)


def _log_sigmoid(z):
    return jnp.minimum(z, 0.0) - jnp.log1p(jnp.exp(-jnp.abs(z)))


def _sigmoid(z):
    return 1.0 / (1.0 + jnp.exp(-z))


def _ffn_kernel(*refs, has_mix, final):
    it = iter(refs)
    x_ref, g_ref, wg_ref, wu_ref, wd_ref = (next(it) for _ in range(5))
    if has_mix:
        of_ref, on_ref, wo_ref = (next(it) for _ in range(3))
    if final:
        gf_ref = next(it)
    o_ref, h_sc, acc_sc = (next(it) for _ in range(3))
    j = pl.program_id(1)

    @pl.when(j == 0)
    def _():
        x = x_ref[...]
        if has_mix:
            x = x + _dot(of_ref[...], wo_ref[:D_FOX, :]) + _dot(on_ref[...], wo_ref[D_FOX:, :])
            o_ref[...] = x
        h_sc[...] = _rms(x, g_ref[...]).astype(BF16)
        acc_sc[...] = jnp.zeros_like(acc_sc)

    h = h_sc[...]
    gate = _dot(h, wg_ref[...])
    up = _dot(h, wu_ref[...])
    a = gate * _sigmoid(gate) * up
    acc_sc[...] += _dot(a.astype(BF16), wd_ref[...])

    @pl.when(j == pl.num_programs(1) - 1)
    def _():
        y = (o_ref[...] if has_mix else x_ref[...]) + 0.5 * acc_sc[...]
        if final:
            y = _rms(y, gf_ref[...])
        o_ref[...] = y


def _ffn(x, g, wg, wu, wd, mix=None, gf=None, tm=512, tf=512):
    n = x.shape[0]
    tm = min(tm, n)
    grid = (n // tm, D_FF // tf)
    row = lambda i, j: (i, 0)
    in_specs = [pl.BlockSpec((tm, D_MODEL), row), _const_spec((1, D_MODEL)),
                pl.BlockSpec((D_MODEL, tf), lambda i, j: (0, j)), pl.BlockSpec((D_MODEL, tf), lambda i, j: (0, j)),
                pl.BlockSpec((tf, D_MODEL), lambda i, j: (j, 0))]
    args = [x, g, wg, wu, wd]
    if mix is not None:
        o_f, o_n, w_out = mix
        in_specs += [pl.BlockSpec((tm, D_FOX), row), pl.BlockSpec((tm, D_NSA), row),
                     _const_spec((D_FOX + D_NSA, D_MODEL))]
        args += [o_f, o_n, w_out]
    if gf is not None:
        in_specs.append(_const_spec((1, D_MODEL)))
        args.append(gf)
    return pl.pallas_call(
        functools.partial(_ffn_kernel, has_mix=mix is not None, final=gf is not None),
        out_shape=jax.ShapeDtypeStruct((n, D_MODEL), F32),
        grid=grid, in_specs=in_specs, out_specs=pl.BlockSpec((tm, D_MODEL), row),
        scratch_shapes=[pltpu.VMEM((tm, D_MODEL), BF16), pltpu.VMEM((tm, D_MODEL), F32)],
        compiler_params=_cparams(("parallel", "arbitrary")), name="ffn")(*args)


def _proj_kernel(*refs, with_cmp):
    x_ref, g_ref, wa_ref, wb_ref, b_ref, bt_ref, qf_ref, qn_ref, misc_ref, fkvt_ref, nkvt_ref, wint_ref, misct_ref = (
        refs[:13])
    tm = x_ref.shape[0]
    h = _rms(x_ref[...], g_ref[...]).astype(BF16)
    qf_ref[...] = (_dot(h, wa_ref[:, _A_QF:_A_QN]) * SCALE).astype(BF16)
    qn_ref[...] = (_dot(h, wa_ref[:, _A_QN:_A_KCVC]) * SCALE).astype(BF16)
    z = _dot(h, wa_ref[:, _A_MISC:_A_END]) + b_ref[...]
    lane = lax.broadcasted_iota(jnp.int32, z.shape, 1)
    misc_ref[...] = jnp.where(lane < N_FOX, _log_sigmoid(z), _sigmoid(z))
    fkvt_ref[...] = _dot_nt(wb_ref[_B_FKV:_B_NKV, :], h)
    nkvt_ref[...] = _dot_nt(wb_ref[_B_NKV:_B_WIN, :], h)
    wint_ref[...] = _dot_nt(wb_ref[_B_WIN:_B_MISC, :], h)
    zt = _dot_nt(wb_ref[_B_MISC:_B_END, :], h) + bt_ref[...]
    sub = lax.broadcasted_iota(jnp.int32, zt.shape, 0)
    misct_ref[...] = jnp.where(sub < N_FOX, _log_sigmoid(zt), _sigmoid(zt))
    if with_cmp:
        xcmp_ref, tr_sc = refs[13:]
        kcvc = _dot(h, wa_ref[:, _A_KCVC:_A_MISC])
        nch = 2 * D_KV // LANES
        for c in range(nch):
            tr_sc[c] = kcvc[:, c * LANES:(c + 1) * LANES]
        for l in range(STRIDE):
            for c in range(nch):
                rows = tr_sc[c, pl.ds(l, tm // STRIDE, stride=STRIDE), :]
                for i in range(2):
                    xcmp_ref[2 * c + i, :, l * HEAD_DIM:(l + 1) * HEAD_DIM] = (
                        rows[:, i * HEAD_DIM:(i + 1) * HEAD_DIM])


def _proj(x, g, wa, wb, b, bt, with_cmp, tm=256):
    bsz, t, _ = x.shape
    tm = min(tm, t)
    nt = t // tm
    row = lambda bi, i: (bi, i, 0)
    col = lambda bi, i: (bi, 0, i)
    out_shape = [jax.ShapeDtypeStruct((bsz, t, D_FOX), BF16), jax.ShapeDtypeStruct((bsz, t, D_NSA), BF16),
                 jax.ShapeDtypeStruct((bsz, t, MISC_W), F32),
                 jax.ShapeDtypeStruct((bsz, 2 * D_FOX, t), F32), jax.ShapeDtypeStruct((bsz, 4 * D_KV, t), F32),
                 jax.ShapeDtypeStruct((bsz, 2 * D_KV, t), F32), jax.ShapeDtypeStruct((bsz, MISC_W, t), F32)]
    out_specs = [pl.BlockSpec((None, tm, D_FOX), row), pl.BlockSpec((None, tm, D_NSA), row),
                 pl.BlockSpec((None, tm, MISC_W), row),
                 pl.BlockSpec((None, 2 * D_FOX, tm), col), pl.BlockSpec((None, 4 * D_KV, tm), col),
                 pl.BlockSpec((None, 2 * D_KV, tm), col), pl.BlockSpec((None, MISC_W, tm), col)]
    scratch = []
    if with_cmp:
        out_shape.append(jax.ShapeDtypeStruct((2 * N_KV, bsz * t // STRIDE, CMP_ROW), F32))
        out_specs.append(pl.BlockSpec((2 * N_KV, tm // STRIDE, CMP_ROW), lambda bi, i: (0, bi * nt + i, 0)))
        scratch.append(pltpu.VMEM((2 * D_KV // LANES, tm, LANES), F32))
    return pl.pallas_call(
        functools.partial(_proj_kernel, with_cmp=with_cmp), out_shape=tuple(out_shape), grid=(bsz, nt),
        in_specs=[pl.BlockSpec((None, tm, D_MODEL), row), _const_spec((1, D_MODEL)),
                  _const_spec((D_MODEL, _A_END)), _const_spec((_B_END, D_MODEL)),
                  _const_spec((1, MISC_W)), _const_spec((MISC_W, 1))],
        out_specs=tuple(out_specs), scratch_shapes=scratch,
        compiler_params=_cparams(("parallel", "parallel")), name="proj")(x, g, wa, wb, b, bt)


def _cumsum_kernel(lf_ref, tri_ref, c_ref):
    c_ref[...] = _dot3(lf_ref[...], tri_ref[...])


def _cumsum(misct):
    b, _, t = misct.shape
    tri = jnp.triu(jnp.ones((t, t), BF16))
    return pl.pallas_call(
        _cumsum_kernel, out_shape=jax.ShapeDtypeStruct((b, N_FOX, t), F32), grid=(b,),
        in_specs=[pl.BlockSpec((None, N_FOX, t), lambda i: (i, 0, 0)), _const_spec((t, t))],
        out_specs=pl.BlockSpec((None, N_FOX, t), lambda i: (i, 0, 0)),
        compiler_params=_cparams(("parallel",)), name="fox_cumsum")(misct, tri)


def _fox_kernel(q_ref, k_ref, v_ref, c_ref, o_ref, kb_sc, vb_sc, cb_sc, *, tq, nh):
    qi = pl.program_id(2)
    t = k_ref.shape[1]
    tk = tq
    nk = t // tk

    @pl.when(qi == 0)
    def _():
        for j in range(nk):
            cols = slice(j * tk, (j + 1) * tk)
            cb_sc[j] = c_ref[:, cols]
            for h in range(nh):
                rows = slice(h * HEAD_DIM, (h + 1) * HEAD_DIM)
                kb_sc[j, h] = k_ref[rows, cols].astype(BF16)
                vb_sc[j, h] = _aug_values(v_ref[rows, cols])

    q = q_ref[...]
    qs = [q[:, h * HEAD_DIM:(h + 1) * HEAD_DIM] for h in range(nh)]
    causal = (lax.broadcasted_iota(jnp.int32, (tq, tk), 1) <= lax.broadcasted_iota(jnp.int32, (tq, tk), 0))

    def step(kj, carry, diagonal):
        ck = cb_sc[kj]
        out = []
        for h in range(nh):
            s = _dot(qs[h], kb_sc[kj, h]) - ck[h:h + 1, :]
            if diagonal:
                s = jnp.where(causal, s, NEG)
            out.append(_flash_update(s, *carry[h], vb_sc[kj, h]))
        return tuple(out)

    init = tuple((jnp.full((tq, 1), NEG, F32), jnp.zeros((tq, V_AUG), F32)) for _ in range(nh))
    carry = lax.fori_loop(0, qi, functools.partial(step, diagonal=False), init)
    carry = step(qi, carry, True)
    o_ref[...] = jnp.concatenate([_flash_out(acc) for _, acc in carry], axis=1).astype(BF16)


def _fox_prompt(qf, fkvt, crow, tq=256, nh=4):
    b, t, _ = qf.shape
    tq = min(tq, t)
    nk = t // tq
    ng = N_FOX // nh
    w = nh * HEAD_DIM
    crow = crow.reshape(b, ng, nh, t)
    return pl.pallas_call(
        functools.partial(_fox_kernel, tq=tq, nh=nh),
        out_shape=jax.ShapeDtypeStruct((b, t, D_FOX), BF16),
        grid=(b, ng, t // tq),
        in_specs=[pl.BlockSpec((None, tq, w), lambda bi, hg, qi: (bi, qi, hg)),
                  pl.BlockSpec((None, w, t), lambda bi, hg, qi: (bi, hg, 0)),
                  pl.BlockSpec((None, w, t), lambda bi, hg, qi: (bi, ng + hg, 0)),
                  pl.BlockSpec((None, None, nh, t), lambda bi, hg, qi: (bi, hg, 0, 0))],
        out_specs=pl.BlockSpec((None, tq, w), lambda bi, hg, qi: (bi, qi, hg)),
        scratch_shapes=[pltpu.VMEM((nk, nh, HEAD_DIM, tq), BF16), pltpu.VMEM((nk, nh, V_AUG, tq), BF16),
                        pltpu.VMEM((nk, nh, tq), F32)],
        compiler_params=_cparams(("parallel", "parallel", "arbitrary")), name="fox_prompt")(qf, fkvt, fkvt, crow)


def _compress_rows(x, pe_ref, w1_ref, w2_ref):
    r = x.shape[0]
    xa = (x + pe_ref[0:1, :]).astype(BF16)
    xb = (x + pe_ref[1:2, :]).astype(BF16)
    ha = _dot(xa, w1_ref[:CMP_ROW, :])
    hb = _dot(xb, w1_ref[CMP_ROW:, :])
    hid = ha + pltpu.roll(hb, r - 1, 0)
    gl = 0.5 * hid * (1.0 + jnp.tanh(0.7978845608028654 * (hid + 0.044715 * (hid * hid * hid))))
    return _dot(gl.astype(BF16), w2_ref[...])


def _compress_kernel(x_ref, pe_ref, w1_ref, w2_ref, o_ref):
    o_ref[...] = _compress_rows(x_ref[...], pe_ref, w1_ref, w2_ref).astype(BF16)


def _compress_prompt(xcmp, pe2, w1, w2, b, t):
    r = t // STRIDE
    return pl.pallas_call(
        _compress_kernel,
        out_shape=jax.ShapeDtypeStruct((b, 2 * N_KV, r, HEAD_DIM), BF16),
        grid=(b, 2 * N_KV),
        in_specs=[pl.BlockSpec((None, r, CMP_ROW), lambda bi, j: (j, bi, 0)),
                  pl.BlockSpec((None, 2, CMP_ROW), lambda bi, j: (j // N_KV, 0, 0)),
                  pl.BlockSpec((None, L_CMP * HEAD_DIM, LANES), lambda bi, j: (j // N_KV, 0, 0)),
                  pl.BlockSpec((None, LANES, HEAD_DIM), lambda bi, j: (j // N_KV, 0, 0))],
        out_specs=pl.BlockSpec((None, None, r, HEAD_DIM), lambda bi, j: (bi, j, 0, 0)),
        compiler_params=_cparams(("parallel", "parallel")), name="nsa_compress")(xcmp, pe2, w1, w2)


def _slope_col(slopes_ref, g, hrow):
    s0, s1, s2, s3 = (slopes_ref[g, h] for h in range(HPG))
    return jnp.where(hrow == 0, s0, jnp.where(hrow == 1, s1, jnp.where(hrow == 2, s2, s3)))


def _select_blocks(p_slc, cur, nsb, axis):
    jb = lax.broadcasted_iota(jnp.int32, p_slc.shape, axis)
    valid = jb <= cur
    forced = valid & ((jb == 0) | ((cur - jb) < N_LOCAL))
    score = jnp.where(valid, p_slc, -jnp.inf)
    score = jnp.where(forced, jnp.inf, score)
    rank = jnp.zeros(p_slc.shape, jnp.int32)
    for i in range(nsb):
        ci = score[:, i:i + 1] if axis == 1 else score[i:i + 1, :]
        beats = (ci > score) | ((ci == score) & (jb > i))
        rank = rank + jnp.where(beats, 1, 0)
    return jnp.where((rank < N_SEL) & (jb < nsb), 1.0, 0.0)


def _cmp_branch(q4, kc, vc, slope, qpos, tq):
    ncp = kc.shape[0]
    s = _dot_nt(q4, kc)
    cend = lax.broadcasted_iota(jnp.int32, (1, ncp), 1) * STRIDE + (L_CMP - 1)
    s = s - slope * (qpos.astype(F32) - cend.astype(F32))
    pc = _masked_softmax(s, cend <= qpos)
    o_c = _dot(pc.astype(BF16), vc)
    return o_c, pc[0:tq] + pc[tq:2 * tq] + pc[2 * tq:3 * tq] + pc[3 * tq:4 * tq]


def _gate_col(misc, br, g):
    cols = [_lane_pick(misc, GATE_OFF + br * N_NSA + g * HPG + h) for h in range(HPG)]
    return jnp.concatenate(cols, axis=0)


def _slc_matrix(nc, nsb, ncp, width):
    r, c = L_SEL // STRIDE, L_CMP // STRIDE
    offs, counts = np.unique((np.arange(r)[:, None] - np.arange(c)[None, :]).ravel(), return_counts=True)
    m = np.zeros((ncp, width), np.float32)
    for j in range(nsb):
        for off, cnt in zip(offs, counts):
            n = r * j + off
            if 0 <= n < nc:
                m[n, j] += cnt
    return jnp.asarray(m, BF16)


def _alibi_slopes():
    base = 2.0 ** (-8.0 * np.arange(1, N_NSA + 1, dtype=np.float32) / N_NSA)
    return jnp.asarray(base.astype(np.float32).reshape(HPG, N_KV).T)


def _nsa_kernel(slopes_ref, q_ref, misc_ref, cmp_ref, ks_ref, vs_ref, kw_ref, vw_ref, m_ref, e_ref, o_ref,
                ksb, vsb, kwb, vwb, *, tq, tk, nsb):
    g = pl.program_id(1)
    qi = pl.program_id(2)
    t = ks_ref.shape[1]
    nk = t // tk
    wch = LANES
    nwc = t // wch

    @pl.when(qi == 0)
    def _():
        for j in range(nk):
            ksb[j] = ks_ref[:, j * tk:(j + 1) * tk].astype(BF16)
            vsb[j] = _aug_values(vs_ref[:, j * tk:(j + 1) * tk])
        for j in range(nwc):
            kwb[j] = kw_ref[:, j * wch:(j + 1) * wch].astype(BF16)
            vwb[j] = _aug_values(vw_ref[:, j * wch:(j + 1) * wch])

    a = qi * tq
    rows = HPG * tq
    row = lax.broadcasted_iota(jnp.int32, (rows, 1), 0)
    qpos = a + row % tq
    hrow = row // tq
    qpos1 = a + lax.broadcasted_iota(jnp.int32, (tq, 1), 0)
    misc = misc_ref[...]
    slope = _slope_col(slopes_ref, g, hrow)
    q4 = jnp.concatenate([q_ref[:, h * HEAD_DIM:(h + 1) * HEAD_DIM] for h in range(HPG)], axis=0)

    o_c, pg = _cmp_branch(q4, cmp_ref[g], cmp_ref[N_KV + g], slope, qpos, tq)
    hi, mid, lo = _split3(pg)
    mt = m_ref[...]
    p_slct = _dot_nt(mt, hi) + _dot_nt(mt, mid) + _dot_nt(mt, lo)
    cur = (a + lax.broadcasted_iota(jnp.int32, (1, tq), 1)) // L_SEL
    selt = _select_blocks(p_slct, cur, nsb, 0)
    sel = jnp.concatenate([selt, jnp.zeros((LANES - selt.shape[0], tq), F32)], axis=0).T.astype(BF16)

    def step(kj, carry, diagonal):
        kpos = kj * tk + lax.broadcasted_iota(jnp.int32, (1, tk), 1)
        keep = _dot(sel, e_ref[kj]) > 0.5
        if diagonal:
            keep = keep & (kpos <= qpos1)
        s4 = _dot(q4, ksb[kj])
        drel = (kpos - a).astype(F32)
        out = []
        for h in range(HPG):
            s = s4[h * tq:(h + 1) * tq] + slopes_ref[g, h] * drel
            out.append(_flash_update(jnp.where(keep, s, NEG), *carry[h], vsb[kj]))
        return tuple(out)

    init = tuple((jnp.full((tq, 1), NEG, F32), jnp.zeros((tq, V_AUG), F32)) for _ in range(HPG))
    kd = a // tk
    carry = lax.fori_loop(0, kd, functools.partial(step, diagonal=False), init)
    carry = step(kd, carry, True)
    o_s = jnp.concatenate([_flash_out(acc) for _, acc in carry], axis=0)

    full = WINDOW // wch + 1
    nwin = min(full, nwc)
    c0 = qi + 1 - nwin
    rr = lax.broadcasted_iota(jnp.int32, (tq, wch), 0)
    cc = lax.broadcasted_iota(jnp.int32, (tq, wch), 1)
    s_h = [[] for _ in range(HPG)]
    cis = []
    for i in range(nwin):
        exists = c0 + i >= 0
        ci = jnp.maximum(c0 + i, 0)
        cis.append(ci)
        s4 = _dot(q4, kwb[ci])
        drel = ((c0 + i) * wch - a + lax.broadcasted_iota(jnp.int32, (1, wch), 1)).astype(F32)
        if i == nwin - 1:
            keep = cc <= rr
        elif i == 0 and nwin == full:
            keep = (cc > rr) & exists
        else:
            keep = exists
        for h in range(HPG):
            s_h[h].append(jnp.where(keep, s4[h * tq:(h + 1) * tq] + slopes_ref[g, h] * drel, NEG))
    o_w = []
    for h in range(HPG):
        m = functools.reduce(jnp.maximum, [jnp.max(s, axis=1, keepdims=True) for s in s_h[h]])
        acc = _dot_nt(jnp.exp(s_h[h][0] - m).astype(BF16), vwb[cis[0]])
        for i in range(1, nwin):
            acc = acc + _dot_nt(jnp.exp(s_h[h][i] - m).astype(BF16), vwb[cis[i]])
        o_w.append(_flash_out(acc))
    o_w = jnp.concatenate(o_w, axis=0)

    o = _gate_col(misc, 0, g) * o_c + _gate_col(misc, 1, g) * o_s + _gate_col(misc, 2, g) * o_w
    o_ref[...] = jnp.concatenate([o[h * tq:(h + 1) * tq] for h in range(HPG)], axis=1).astype(BF16)


def _nsa_prompt(qn, misc, cmp, nkvt, wint, tq=128, tk=256):
    b, t, _ = qn.shape
    tq = min(tq, t)
    tk = min(tk, t)
    assert tk % tq == 0 and tq == LANES
    ncp = t // STRIDE
    nc = (t - L_CMP) // STRIDE + 1
    nsb = -(-t // L_SEL)
    assert nsb <= LANES
    nsbp = -(-nsb // SUBLANES) * SUBLANES
    m = _slc_matrix(nc, nsb, ncp, nsbp).T
    e = (np.arange(t)[None, :] // L_SEL == np.arange(LANES)[:, None]).astype(np.float32)
    e = jnp.asarray(e.reshape(LANES, t // tk, tk).transpose(1, 0, 2), BF16)
    hw = HPG * HEAD_DIM
    grid_spec = pl.GridSpec(
        grid=(b, N_KV, t // tq),
        in_specs=[pl.BlockSpec(memory_space=pltpu.SMEM),
                  pl.BlockSpec((None, tq, hw), lambda bi, g, qi: (bi, qi, g)),
                  pl.BlockSpec((None, tq, MISC_W), lambda bi, g, qi: (bi, qi, 0)),
                  pl.BlockSpec((None, 2 * N_KV, ncp, HEAD_DIM), lambda bi, g, qi: (bi, 0, 0, 0)),
                  pl.BlockSpec((None, HEAD_DIM, t), lambda bi, g, qi: (bi, 2 * N_KV + g, 0)),
                  pl.BlockSpec((None, HEAD_DIM, t), lambda bi, g, qi: (bi, 3 * N_KV + g, 0)),
                  pl.BlockSpec((None, HEAD_DIM, t), lambda bi, g, qi: (bi, g, 0)),
                  pl.BlockSpec((None, HEAD_DIM, t), lambda bi, g, qi: (bi, N_KV + g, 0)),
                  _const_spec((nsbp, ncp)), _const_spec((t // tk, LANES, tk))],
        out_specs=pl.BlockSpec((None, tq, hw), lambda bi, g, qi: (bi, qi, g)),
        scratch_shapes=[pltpu.VMEM((t // tk, HEAD_DIM, tk), BF16), pltpu.VMEM((t // tk, V_AUG, tk), BF16),
                        pltpu.VMEM((t // LANES, HEAD_DIM, LANES), BF16),
                        pltpu.VMEM((t // LANES, V_AUG, LANES), BF16)])
    return pl.pallas_call(
        functools.partial(_nsa_kernel, tq=tq, tk=tk, nsb=nsb),
        out_shape=jax.ShapeDtypeStruct((b, t, D_NSA), BF16), grid_spec=grid_spec,
        compiler_params=_cparams(("parallel", "parallel", "arbitrary")), name="nsa_prompt",
    )(_alibi_slopes(), qn, misc, cmp, nkvt, nkvt, wint, wint, m, e)


def _fox_dec_kernel(*refs, n_new, ppb):
    pt_ref, q_ref = refs[:2]
    kv_refs = refs[2:2 + ppb]
    lf_refs = refs[2 + ppb:2 + 2 * ppb]
    kvn_ref, lfn_ref, gt_ref, le_ref, o_ref, qbd_sc, m_sc, l_sc, r_sc, acc_sc = refs[2 + 2 * ppb:]
    p = pl.program_id(1)
    page = kvn_ref.shape[2]
    rows = n_new * N_FOX

    def tile_heads(x):
        return jnp.concatenate([x] * n_new, axis=0)

    def accumulate(kt, vt, bias, mask):
        s = _dot(qbd_sc[...], kt) + bias
        if mask is not None:
            s = jnp.where(mask, s, NEG)
        m, alpha, l, pr = _online_softmax_step(s, m_sc[...], l_sc[...])
        m_sc[...] = m
        l_sc[...] = l
        acc_sc[...] = alpha * acc_sc[...] + _dot_nt(pr.astype(BF16), vt)

    @pl.when(p == 0)
    def _():
        q = q_ref[...]
        rep = jnp.concatenate([jnp.broadcast_to(q[t:t + 1, :], (N_FOX, D_FOX)) for t in range(n_new)], axis=0)
        rr = lax.broadcasted_iota(jnp.int32, (rows, D_FOX), 0)
        ll = lax.broadcasted_iota(jnp.int32, (rows, D_FOX), 1)
        qbd_sc[...] = jnp.where(ll // HEAD_DIM == rr % N_FOX, rep, 0.0).astype(BF16)
        m_sc[...] = jnp.full(m_sc.shape, NEG, F32)
        l_sc[...] = jnp.zeros(l_sc.shape, F32)
        r_sc[...] = jnp.zeros(r_sc.shape, F32)
        acc_sc[...] = jnp.zeros(acc_sc.shape, F32)
        cn = _dot3(lfn_ref[...], le_ref[...])
        tkey = lax.broadcasted_iota(jnp.int32, (rows, page), 1)
        tqry = lax.broadcasted_iota(jnp.int32, (rows, page), 0) // N_FOX
        accumulate(kvn_ref[0].astype(BF16), kvn_ref[1].astype(BF16), -tile_heads(cn),
                   (tkey <= tqry) & (tkey < n_new))

    later = r_sc[...]
    suffix = []
    for lf_ref in lf_refs:
        lf = lf_ref[...]
        suffix.append(_dot3(lf, gt_ref[...]) + later)
        later = later + jnp.sum(lf, axis=1, keepdims=True)
    r_sc[...] = later
    accumulate(jnp.concatenate([r[0].astype(BF16) for r in kv_refs], axis=1),
               jnp.concatenate([r[1].astype(BF16) for r in kv_refs], axis=1),
               tile_heads(jnp.concatenate(suffix, axis=1)), None)

    @pl.when(p == pl.num_programs(1) - 1)
    def _():
        o = acc_sc[...] / jnp.maximum(l_sc[...], 1e-30)
        rr = lax.broadcasted_iota(jnp.int32, o.shape, 0)
        ll = lax.broadcasted_iota(jnp.int32, o.shape, 1)
        o = jnp.where(ll // HEAD_DIM == rr % N_FOX, o, 0.0)
        orow = lax.broadcasted_iota(jnp.int32, (SQ, D_FOX), 0)
        out = jnp.zeros((SQ, D_FOX), F32)
        for t in range(n_new):
            out = jnp.where(orow == t, jnp.sum(o[t * N_FOX:(t + 1) * N_FOX], axis=0, keepdims=True), out)
        o_ref[...] = out


def _fox_decode(pt_flat, qf_pad, cache_kvt, cache_lft, kvt_new, lft_new, n_pages, n_new):
    db = qf_pad.shape[0]
    page = cache_kvt.shape[3]
    gt = np.tril(np.ones((page, page), np.float32), -1)
    le = np.triu(np.ones((page, page), np.float32))
    ppb = next(c for c in (8, 4, 2, 1) if n_pages % c == 0)
    pidx = lambda i: (lambda b, p, pt: (pt[b * n_pages + n_pages - 1 - (p * ppb + i)], 0, 0, 0))
    pidx3 = lambda i: (lambda b, p, pt: (pt[b * n_pages + n_pages - 1 - (p * ppb + i)], 0, 0))
    rows = n_new * N_FOX
    grid_spec = pltpu.PrefetchScalarGridSpec(
        num_scalar_prefetch=1, grid=(db, n_pages // ppb),
        in_specs=[pl.BlockSpec((None, SQ, D_FOX), lambda b, p, pt: (b, 0, 0))]
        + [pl.BlockSpec((None, 2, D_FOX, page), pidx(i)) for i in range(ppb)]
        + [pl.BlockSpec((None, N_FOX, page), pidx3(i)) for i in range(ppb)]
        + [pl.BlockSpec((None, 2, D_FOX, page), lambda b, p, pt: (b, 0, 0, 0)),
           pl.BlockSpec((None, N_FOX, page), lambda b, p, pt: (b, 0, 0)),
           _const_spec((page, page)), _const_spec((page, page))],
        out_specs=pl.BlockSpec((None, SQ, D_FOX), lambda b, p, pt: (b, 0, 0)),
        scratch_shapes=[pltpu.VMEM((rows, D_FOX), BF16), pltpu.VMEM((rows, 1), F32), pltpu.VMEM((rows, 1), F32),
                        pltpu.VMEM((N_FOX, 1), F32), pltpu.VMEM((rows, D_FOX), F32)])
    return pl.pallas_call(
        functools.partial(_fox_dec_kernel, n_new=n_new, ppb=ppb),
        out_shape=jax.ShapeDtypeStruct((db, SQ, D_FOX), F32), grid_spec=grid_spec,
        compiler_params=_cparams(("parallel", "arbitrary")), name="fox_decode",
    )(pt_flat, qf_pad, *([cache_kvt] * ppb), *([cache_lft] * ppb), kvt_new, lft_new,
      jnp.asarray(gt, BF16), jnp.asarray(le, BF16))


def _nsa_pages_kernel(*refs, ppb):
    pg_refs = refs[1:1 + ppb]
    new_ref, pe_ref, w1_ref, w2_ref, ksvs_ref, cmp_ref, tr_sc, x_sc = refs[1 + ppb:]
    p = pl.program_id(1)
    npg = pl.num_programs(1) - 1
    page = new_ref.shape[1]
    rpp = page // STRIDE
    nch = D_KV // LANES

    @pl.when(p < npg)
    def _():
        ksvs_ref[...] = jnp.concatenate(
            [jnp.concatenate([r[2], r[3]], axis=0) for r in pg_refs], axis=1).astype(BF16)
        for k, pg_ref in enumerate(pg_refs):
            for s in range(2):
                for c in range(nch):
                    tr_sc[(k * 2 + s) * nch + c] = pg_ref[s, c * LANES:(c + 1) * LANES, :].T
            r0 = pl.multiple_of((p * ppb + k) * rpp, rpp)
            for l in range(STRIDE):
                for c in range(2 * nch):
                    rows = tr_sc[k * 2 * nch + c, pl.ds(l, rpp, stride=STRIDE), :]
                    for i in range(2):
                        x_sc[2 * c + i, pl.ds(r0, rpp), l * HEAD_DIM:(l + 1) * HEAD_DIM] = (
                            rows[:, i * HEAD_DIM:(i + 1) * HEAD_DIM])

    @pl.when(p == npg)
    def _():
        pad = [jnp.zeros((2 * D_KV, (ppb - 1) * page), F32)] if ppb > 1 else []
        ksvs_ref[...] = jnp.concatenate([new_ref[...]] + pad, axis=1).astype(BF16)
        for j in range(2 * N_KV):
            s = j // N_KV
            cmp_ref[j] = _compress_rows(x_sc[j], pe_ref.at[s], w1_ref.at[s], w2_ref.at[s]).astype(BF16)


def _nsa_pages(pt_flat, cache_nsat, ksvst_new, pe2, w1, w2, n_pages):
    db = ksvst_new.shape[0]
    page = cache_nsat.shape[3]
    past = n_pages * page
    rows = past // STRIDE
    ppb = next(c for c in (4, 2, 1) if n_pages % c == 0)
    nsteps = n_pages // ppb
    pidx = lambda i: (lambda b, p, pt: (pt[b * n_pages + jnp.minimum(p * ppb + i, n_pages - 1)], 0, 0, 0))
    grid_spec = pltpu.PrefetchScalarGridSpec(
        num_scalar_prefetch=1, grid=(db, nsteps + 1),
        in_specs=[pl.BlockSpec((None, 4, D_KV, page), pidx(i)) for i in range(ppb)]
        + [pl.BlockSpec((None, 2 * D_KV, page), lambda b, p, pt: (b, 0, 0)),
           _const_spec((2, 2, CMP_ROW)), _const_spec((2, L_CMP * HEAD_DIM, LANES)),
           _const_spec((2, LANES, HEAD_DIM))],
        out_specs=(pl.BlockSpec((None, 2 * D_KV, ppb * page), lambda b, p, pt: (b, 0, p)),
                   pl.BlockSpec((None, 2 * N_KV, rows, HEAD_DIM), lambda b, p, pt: (b, 0, 0, 0))),
        scratch_shapes=[pltpu.VMEM((ppb * 2 * D_KV // LANES, page, LANES), F32),
                        pltpu.VMEM((2 * N_KV, rows, CMP_ROW), F32)])
    return pl.pallas_call(
        functools.partial(_nsa_pages_kernel, ppb=ppb),
        out_shape=(jax.ShapeDtypeStruct((db, 2 * D_KV, past + ppb * page), BF16),
                   jax.ShapeDtypeStruct((db, 2 * N_KV, rows, HEAD_DIM), BF16)),
        grid_spec=grid_spec, compiler_params=_cparams(("parallel", "arbitrary")), name="nsa_pages",
    )(pt_flat, *([cache_nsat] * ppb), ksvst_new, pe2, w1, w2)


def _nsa_dec_kernel(slopes_ref, q_ref, misc_ref, cmp_ref, ksvs_ref, win_ref, winn_ref, m_ref, e_ref, o_ref,
                    selx_sc, *, past, nsb):
    tq = SQ
    rows = HPG * tq
    row = lax.broadcasted_iota(jnp.int32, (rows, 1), 0)
    qpos = past + row % tq
    qposf = qpos.astype(F32)
    hrow = row // tq
    misc = misc_ref[...]
    q = q_ref[...]
    cur = (past + lax.broadcasted_iota(jnp.int32, (tq, 1), 0)) // L_SEL
    nkeys = ksvs_ref.shape[1]

    o_cs, sels, q4s, slopes = [], [], [], []
    for g in range(N_KV):
        slope = _slope_col(slopes_ref, g, hrow)
        q4 = jnp.concatenate(
            [q[:, (g * HPG + h) * HEAD_DIM:(g * HPG + h + 1) * HEAD_DIM] for h in range(HPG)], axis=0).astype(BF16)
        o_c, pg = _cmp_branch(q4, cmp_ref[g], cmp_ref[N_KV + g], slope, qpos, tq)
        sels.append(_select_blocks(_dot3(pg, m_ref[...]), cur, nsb, 1))
        o_cs.append(o_c)
        q4s.append(q4)
        slopes.append(slope)
    selx_sc[...] = _dot(jnp.concatenate(sels, axis=0).astype(BF16), e_ref[...])

    kpos = lax.broadcasted_iota(jnp.int32, (1, nkeys), 1)
    nwin = win_ref.shape[2]
    wpos = (past - nwin) + lax.broadcasted_iota(jnp.int32, (1, nwin + winn_ref.shape[2]), 1)
    outs = []
    for g in range(N_KV):
        q4, slope = q4s[g], slopes[g]
        gr = slice(g * HEAD_DIM, (g + 1) * HEAD_DIM)
        s = _dot(q4, ksvs_ref[gr, :]) - slope * (qposf - kpos.astype(F32))
        selx = selx_sc[g * tq:(g + 1) * tq, :]
        mask = (jnp.concatenate([selx] * HPG, axis=0) > 0.5) & (kpos <= qpos)
        o_s = _dot_nt(_masked_softmax(s, mask).astype(BF16), ksvs_ref[D_KV + g * HEAD_DIM:D_KV + (g + 1) * HEAD_DIM, :])
        kw = jnp.concatenate([win_ref[0, gr, :], winn_ref[0, gr, :]], axis=1).astype(BF16)
        vw = jnp.concatenate([win_ref[1, gr, :], winn_ref[1, gr, :]], axis=1).astype(BF16)
        dist = qpos - wpos
        s = _dot(q4, kw) - slope * dist.astype(F32)
        o_w = _dot_nt(_masked_softmax(s, (dist >= 0) & (dist < WINDOW)).astype(BF16), vw)
        o = _gate_col(misc, 0, g) * o_cs[g] + _gate_col(misc, 1, g) * o_s + _gate_col(misc, 2, g) * o_w
        outs += [o[h * tq:(h + 1) * tq] for h in range(HPG)]
    o_ref[...] = jnp.concatenate(outs, axis=1)


def _nsa_decode(qn_pad, misc_pad, cmp, ksvst, wint_state, wint_new, past, n_new):
    db = qn_pad.shape[0]
    nkeys = ksvst.shape[2]
    ncp = cmp.shape[2]
    tk_all = past + n_new
    nc = (tk_all - L_CMP) // STRIDE + 1
    nsb = -(-tk_all // L_SEL)
    width = -(-nsb // LANES) * LANES
    m = _slc_matrix(nc, nsb, ncp, width)
    e = jnp.asarray(np.arange(nkeys)[None, :] // L_SEL == np.arange(width)[:, None], BF16)
    nwin = wint_state.shape[3]
    nwn = wint_new.shape[3]
    per_b3 = lambda b: (b, 0, 0)
    per_b4 = lambda b: (b, 0, 0, 0)
    grid_spec = pl.GridSpec(
        grid=(db,),
        in_specs=[pl.BlockSpec(memory_space=pltpu.SMEM),
                  pl.BlockSpec((None, SQ, D_NSA), per_b3), pl.BlockSpec((None, SQ, MISC_W), per_b3),
                  pl.BlockSpec((None, 2 * N_KV, ncp, HEAD_DIM), per_b4),
                  pl.BlockSpec((None, 2 * D_KV, nkeys), per_b3),
                  pl.BlockSpec((None, 2, D_KV, nwin), per_b4), pl.BlockSpec((None, 2, D_KV, nwn), per_b4),
                  _const_spec((ncp, width)), _const_spec((width, nkeys))],
        out_specs=pl.BlockSpec((None, SQ, D_NSA), per_b3),
        scratch_shapes=[pltpu.VMEM((N_KV * SQ, nkeys), F32)])
    return pl.pallas_call(
        functools.partial(_nsa_dec_kernel, past=past, nsb=nsb),
        out_shape=jax.ShapeDtypeStruct((db, SQ, D_NSA), F32), grid_spec=grid_spec,
        compiler_params=_cparams(("parallel",)), name="nsa_decode",
    )(_alibi_slopes(), qn_pad, misc_pad, cmp, ksvst, wint_state, wint_new, m, e)


def _prep_w_in(w_in, b_forget, b_gate):
    sizes = (D_FOX, D_FOX, D_FOX, N_FOX, D_NSA) + (D_KV,) * 6 + (3 * N_NSA,)
    offs = np.cumsum((0,) + sizes)
    part = lambda i, j=None: w_in[:, offs[i]:offs[(i if j is None else j) + 1]]
    pad = MISC_W - N_FOX - 3 * N_NSA
    misc = jnp.concatenate([part(3), part(11), jnp.zeros((w_in.shape[0], pad), w_in.dtype)], axis=1)
    wa = jnp.concatenate([part(0), part(4), part(5, 6), misc], axis=1).astype(BF16)
    wb = jnp.concatenate([part(1, 2), part(5, 10), misc], axis=1).T.astype(BF16)
    b = jnp.concatenate([b_forget, b_gate, jnp.zeros((pad,), F32)]).astype(F32)
    return wa, wb, b.reshape(1, MISC_W), b.reshape(MISC_W, 1)


def _pos_minor(x):
    nd = x.ndim
    return jnp.transpose(x, tuple(range(nd - 4)) + (nd - 3, nd - 2, nd - 1, nd - 4))


def _pos_major(x):
    nd = x.ndim
    return jnp.transpose(x, tuple(range(nd - 4)) + (nd - 1, nd - 4, nd - 3, nd - 2))


def kernel(x_prompt, x_sample, cache_fox_kv, cache_fox_logf, cache_nsa_kv, state_win_kv, page_table, norm_ffn1,
           ffn1_gate, ffn1_up, ffn1_down, norm_mix, w_in, b_forget, b_gate, phi_pe, phi_w1, phi_w2, w_out,
           norm_ffn2, ffn2_gate, ffn2_up, ffn2_down, norm_final):
    depth = norm_ffn1.shape[0]
    bsz, t, _ = x_prompt.shape
    db, s_new, _ = x_sample.shape
    n_pool, page = cache_fox_kv.shape[1], cache_fox_kv.shape[2]
    n_pages = page_table.shape[1]
    past = n_pages * page
    wb_len = state_win_kv.shape[2]
    wp = min(WINDOW, t)
    assert s_new <= SQ and wb_len == min(WINDOW, past)

    xp = x_prompt
    xs = x_sample.reshape(1, db * s_new, D_MODEL)
    fox_kvt_pool = _pos_minor(cache_fox_kv).reshape(depth * n_pool, 2, D_FOX, page)
    fox_lft_pool = jnp.swapaxes(cache_fox_logf, 2, 3).reshape(depth * n_pool, N_FOX, page)
    nsa_kvt_pool = _pos_minor(cache_nsa_kv).reshape(depth * n_pool, 4, D_KV, page)
    win_t = _pos_minor(state_win_kv).reshape(depth, db, 2, D_KV, wb_len)
    gfin = norm_final.reshape(1, D_MODEL)
    outs = [[] for _ in range(8)]

    def pad_rows(a):
        a = a.reshape(db, s_new, -1).astype(F32)
        return jnp.pad(a, ((0, 0), (0, SQ - s_new), (0, 0)))

    def new_cols(a_t, lead):
        a_t = jnp.transpose(a_t.reshape(a_t.shape[1], db, s_new), (1, 0, 2))
        return jnp.pad(a_t, ((0, 0), (0, 0), (0, page - s_new))).reshape((db,) + lead + (page,))

    for l in range(depth):
        last = l == depth - 1
        g1 = norm_ffn1[l].reshape(1, D_MODEL)
        gm = norm_mix[l].reshape(1, D_MODEL)
        g2 = norm_ffn2[l].reshape(1, D_MODEL)
        ffn1 = (ffn1_gate[l].astype(BF16), ffn1_up[l].astype(BF16), ffn1_down[l].astype(BF16))
        ffn2 = (ffn2_gate[l].astype(BF16), ffn2_up[l].astype(BF16), ffn2_down[l].astype(BF16))
        wa, wbt, b_r, b_c = _prep_w_in(w_in[l], b_forget[l], b_gate[l])
        wo = w_out[l].astype(BF16)
        pe2 = phi_pe[l].reshape(2, 2, CMP_ROW)
        w1 = phi_w1[l].astype(BF16)
        w2 = phi_w2[l].astype(BF16)
        pt_flat = (page_table + l * n_pool).reshape(-1).astype(jnp.int32)

        xp = _ffn(xp.reshape(bsz * t, D_MODEL), g1, *ffn1).reshape(bsz, t, D_MODEL)
        xs = _ffn(xs.reshape(db * s_new, D_MODEL), g1, *ffn1).reshape(1, db * s_new, D_MODEL)

        qf, qn, misc, fkvt, nkvt, wint, misct, xcmp = _proj(xp, gm, wa, wbt, b_r, b_c, True)
        o_f = _fox_prompt(qf, fkvt, _cumsum(misct))
        cmp = _compress_prompt(xcmp, pe2, w1, w2, bsz, t)
        o_n = _nsa_prompt(qn, misc, cmp, nkvt, wint)
        xp = _ffn(xp.reshape(bsz * t, D_MODEL), g2, *ffn2,
                  mix=(o_f.reshape(bsz * t, D_FOX), o_n.reshape(bsz * t, D_NSA), wo),
                  gf=gfin if last else None).reshape(bsz, t, D_MODEL)
        outs[0].append(_pos_major(fkvt.reshape(bsz, 2, N_FOX, HEAD_DIM, t)))
        outs[1].append(jnp.swapaxes(misct[:, :N_FOX, :], 1, 2))
        outs[2].append(_pos_major(nkvt.reshape(bsz, 4, N_KV, HEAD_DIM, t)))
        outs[3].append(_pos_major(wint[:, :, t - wp:].reshape(bsz, 2, N_KV, HEAD_DIM, wp)))

        qf, qn, misc, fkvt, nkvt, wint, misct = _proj(xs, gm, wa, wbt, b_r, b_c, False)
        o_f = _fox_decode(pt_flat, pad_rows(qf), fox_kvt_pool, fox_lft_pool, new_cols(fkvt, (2, D_FOX)),
                          new_cols(misct[:, :N_FOX, :], (N_FOX,)), n_pages, s_new)
        ksvst, cmp = _nsa_pages(pt_flat, nsa_kvt_pool, new_cols(nkvt[:, 2 * D_KV:, :], (2 * D_KV,)),
                                pe2, w1, w2, n_pages)
        o_n = _nsa_decode(pad_rows(qn), pad_rows(misc), cmp, ksvst, win_t[l], new_cols(wint, (2, D_KV)),
                          past, s_new)
        o_f = o_f[:, :s_new].reshape(db * s_new, D_FOX).astype(BF16)
        o_n = o_n[:, :s_new].reshape(db * s_new, D_NSA).astype(BF16)
        xs = _ffn(xs.reshape(db * s_new, D_MODEL), g2, *ffn2, mix=(o_f, o_n, wo),
                  gf=gfin if last else None).reshape(1, db * s_new, D_MODEL)
        tok_major = lambda a_t, lead: jnp.transpose(a_t[0], (1, 0)).reshape((db, s_new) + lead)
        win_new = tok_major(wint, (2, N_KV, HEAD_DIM))
        outs[4].append(tok_major(fkvt, (2, N_FOX, HEAD_DIM)))
        outs[5].append(tok_major(misct[:, :N_FOX, :], (N_FOX,)))
        outs[6].append(tok_major(nkvt, (4, N_KV, HEAD_DIM)))
        outs[7].append(jnp.concatenate([state_win_kv[l], win_new], axis=1)[:, -wb_len:])

    y_prompt = xp
    y_sample = xs.reshape(db, s_new, D_MODEL)
    return (y_prompt, y_sample) + tuple(jnp.stack(o) for o in outs)
```

```python
import functools

import numpy as np
import jax
import jax.numpy as jnp
from jax import lax
from jax.experimental import pallas as pl
from jax.experimental.pallas import tpu as pltpu

F32 = jnp.float32
BF16 = jnp.bfloat16

D_MODEL = 2048
HEAD_DIM = 64
N_FOX = 16
N_NSA = 16
N_KV = 4
HPG = N_NSA // N_KV
D_FOX = N_FOX * HEAD_DIM
D_NSA = N_NSA * HEAD_DIM
D_KV = N_KV * HEAD_DIM
D_FF = 5632
L_CMP = 32
STRIDE = 16
L_SEL = 64
N_SEL = 16
N_LOCAL = 2
WINDOW = 512
RMS_EPS = 1e-6
NEG = -1e30
SCALE = HEAD_DIM ** -0.5

LANES = 128
SUBLANES = 8
MISC_W = LANES
GATE_OFF = N_FOX
SQ = SUBLANES
CMP_ROW = STRIDE * HEAD_DIM
V_AUG = HEAD_DIM + 2 * SUBLANES

_A_QF = 0
_A_QN = _A_QF + D_FOX
_A_KCVC = _A_QN + D_NSA
_A_MISC = _A_KCVC + 2 * D_KV
_A_END = _A_MISC + MISC_W
_B_FKV = 0
_B_NKV = _B_FKV + 2 * D_FOX
_B_WIN = _B_NKV + 4 * D_KV
_B_MISC = _B_WIN + 2 * D_KV
_B_END = _B_MISC + MISC_W

_VMEM_LIMIT = 56 * 1024 * 1024


def _cparams(sem):
    return pltpu.CompilerParams(dimension_semantics=sem, vmem_limit_bytes=_VMEM_LIMIT)


def _const_spec(shape):
    nd = len(shape)
    return pl.BlockSpec(shape, lambda *_: (0,) * nd, pipeline_mode=pl.Buffered(1))


def _dot(a, b):
    return jnp.dot(a, b, preferred_element_type=F32)


def _dot_nt(a, b):
    return lax.dot_general(a, b, (((1,), (1,)), ((), ())), preferred_element_type=F32)


def _split3(x):
    hi = x.astype(BF16)
    r1 = x - hi.astype(F32)
    mid = r1.astype(BF16)
    lo = (r1 - mid.astype(F32)).astype(BF16)
    return hi, mid, lo


def _dot3(x, m):
    hi, mid, lo = _split3(x)
    return _dot(hi, m) + _dot(mid, m) + _dot(lo, m)


def _rms(x, g):
    return x * lax.rsqrt(jnp.mean(x * x, axis=-1, keepdims=True) + RMS_EPS) * g


def _masked_softmax(s, mask):
    s = jnp.where(mask, s, NEG)
    e = jnp.where(mask, jnp.exp(s - jnp.max(s, axis=-1, keepdims=True)), 0.0)
    return e / jnp.maximum(jnp.sum(e, axis=-1, keepdims=True), 1e-30)


def _online_softmax_step(s, m, l):
    m_new = jnp.maximum(m, jnp.max(s, axis=1, keepdims=True))
    p = jnp.exp(s - m_new)
    alpha = jnp.exp(m - m_new)
    return m_new, alpha, alpha * l + jnp.sum(p, axis=1, keepdims=True), p


def _aug_values(vt):
    extra = lax.broadcasted_iota(jnp.int32, (V_AUG - HEAD_DIM, vt.shape[1]), 0) == 0
    return jnp.concatenate([vt, jnp.where(extra, 1.0, 0.0)], axis=0).astype(BF16)


def _flash_update(s, m, acc, vt_aug):
    m_new = jnp.maximum(m, jnp.max(s, axis=1, keepdims=True))
    p = jnp.exp(s - m_new)
    return m_new, jnp.exp(m - m_new) * acc + _dot_nt(p.astype(BF16), vt_aug)


def _flash_out(acc, gate=1.0):
    return acc[:, :HEAD_DIM] * (gate / jnp.maximum(acc[:, HEAD_DIM:HEAD_DIM + 1], 1e-30))


def _lane_pick(x, idx):
    lane = lax.broadcasted_iota(jnp.int32, x.shape, 1)
    return jnp.sum(jnp.where(lane == idx, x, 0.0), axis=1, keepdims=True)


def _log_sigmoid(z):
    return jnp.minimum(z, 0.0) - jnp.log1p(jnp.exp(-jnp.abs(z)))


def _sigmoid(z):
    return 1.0 / (1.0 + jnp.exp(-z))


def _ffn_kernel(*refs, has_mix, final):
    it = iter(refs)
    x_ref, g_ref, wg_ref, wu_ref, wd_ref = (next(it) for _ in range(5))
    if has_mix:
        of_ref, on_ref, wo_ref = (next(it) for _ in range(3))
    if final:
        gf_ref = next(it)
    o_ref, h_sc, acc_sc = (next(it) for _ in range(3))
    j = pl.program_id(1)

    @pl.when(j == 0)
    def _():
        x = x_ref[...]
        if has_mix:
            x = x + _dot(of_ref[...], wo_ref[:D_FOX, :]) + _dot(on_ref[...], wo_ref[D_FOX:, :])
            o_ref[...] = x
        h_sc[...] = _rms(x, g_ref[...]).astype(BF16)
        acc_sc[...] = jnp.zeros_like(acc_sc)

    h = h_sc[...]
    gate = _dot(h, wg_ref[...])
    up = _dot(h, wu_ref[...])
    a = gate * _sigmoid(gate) * up
    acc_sc[...] += _dot(a.astype(BF16), wd_ref[...])

    @pl.when(j == pl.num_programs(1) - 1)
    def _():
        y = (o_ref[...] if has_mix else x_ref[...]) + 0.5 * acc_sc[...]
        if final:
            y = _rms(y, gf_ref[...])
        o_ref[...] = y


def _ffn(x, g, wg, wu, wd, mix=None, gf=None, tm=512, tf=512):
    n = x.shape[0]
    tm = min(tm, n)
    grid = (n // tm, D_FF // tf)
    row = lambda i, j: (i, 0)
    in_specs = [pl.BlockSpec((tm, D_MODEL), row), _const_spec((1, D_MODEL)),
                pl.BlockSpec((D_MODEL, tf), lambda i, j: (0, j)), pl.BlockSpec((D_MODEL, tf), lambda i, j: (0, j)),
                pl.BlockSpec((tf, D_MODEL), lambda i, j: (j, 0))]
    args = [x, g, wg, wu, wd]
    if mix is not None:
        o_f, o_n, w_out = mix
        in_specs += [pl.BlockSpec((tm, D_FOX), row), pl.BlockSpec((tm, D_NSA), row),
                     _const_spec((D_FOX + D_NSA, D_MODEL))]
        args += [o_f, o_n, w_out]
    if gf is not None:
        in_specs.append(_const_spec((1, D_MODEL)))
        args.append(gf)
    return pl.pallas_call(
        functools.partial(_ffn_kernel, has_mix=mix is not None, final=gf is not None),
        out_shape=jax.ShapeDtypeStruct((n, D_MODEL), F32),
        grid=grid, in_specs=in_specs, out_specs=pl.BlockSpec((tm, D_MODEL), row),
        scratch_shapes=[pltpu.VMEM((tm, D_MODEL), BF16), pltpu.VMEM((tm, D_MODEL), F32)],
        compiler_params=_cparams(("parallel", "arbitrary")), name="ffn")(*args)


def _proj_kernel(*refs, with_cmp):
    x_ref, g_ref, wa_ref, wb_ref, b_ref, bt_ref, qf_ref, qn_ref, misc_ref, fkvt_ref, nkvt_ref, wint_ref, misct_ref = (
        refs[:13])
    tm = x_ref.shape[0]
    h = _rms(x_ref[...], g_ref[...]).astype(BF16)
    qf_ref[...] = (_dot(h, wa_ref[:, _A_QF:_A_QN]) * SCALE).astype(BF16)
    qn_ref[...] = (_dot(h, wa_ref[:, _A_QN:_A_KCVC]) * SCALE).astype(BF16)
    z = _dot(h, wa_ref[:, _A_MISC:_A_END]) + b_ref[...]
    lane = lax.broadcasted_iota(jnp.int32, z.shape, 1)
    misc_ref[...] = jnp.where(lane < N_FOX, _log_sigmoid(z), _sigmoid(z))
    fkvt_ref[...] = _dot_nt(wb_ref[_B_FKV:_B_NKV, :], h)
    nkvt_ref[...] = _dot_nt(wb_ref[_B_NKV:_B_WIN, :], h)
    wint_ref[...] = _dot_nt(wb_ref[_B_WIN:_B_MISC, :], h)
    zt = _dot_nt(wb_ref[_B_MISC:_B_END, :], h) + bt_ref[...]
    sub = lax.broadcasted_iota(jnp.int32, zt.shape, 0)
    misct_ref[...] = jnp.where(sub < N_FOX, _log_sigmoid(zt), _sigmoid(zt))
    if with_cmp:
        xcmp_ref, tr_sc = refs[13:]
        kcvc = _dot(h, wa_ref[:, _A_KCVC:_A_MISC])
        nch = 2 * D_KV // LANES
        for c in range(nch):
            tr_sc[c] = kcvc[:, c * LANES:(c + 1) * LANES]
        for l in range(STRIDE):
            for c in range(nch):
                rows = tr_sc[c, pl.ds(l, tm // STRIDE, stride=STRIDE), :]
                for i in range(2):
                    xcmp_ref[2 * c + i, :, l * HEAD_DIM:(l + 1) * HEAD_DIM] = (
                        rows[:, i * HEAD_DIM:(i + 1) * HEAD_DIM])


def _proj(x, g, wa, wb, b, bt, with_cmp, tm=256):
    bsz, t, _ = x.shape
    tm = min(tm, t)
    nt = t // tm
    row = lambda bi, i: (bi, i, 0)
    col = lambda bi, i: (bi, 0, i)
    out_shape = [jax.ShapeDtypeStruct((bsz, t, D_FOX), BF16), jax.ShapeDtypeStruct((bsz, t, D_NSA), BF16),
                 jax.ShapeDtypeStruct((bsz, t, MISC_W), F32),
                 jax.ShapeDtypeStruct((bsz, 2 * D_FOX, t), F32), jax.ShapeDtypeStruct((bsz, 4 * D_KV, t), F32),
                 jax.ShapeDtypeStruct((bsz, 2 * D_KV, t), F32), jax.ShapeDtypeStruct((bsz, MISC_W, t), F32)]
    out_specs = [pl.BlockSpec((None, tm, D_FOX), row), pl.BlockSpec((None, tm, D_NSA), row),
                 pl.BlockSpec((None, tm, MISC_W), row),
                 pl.BlockSpec((None, 2 * D_FOX, tm), col), pl.BlockSpec((None, 4 * D_KV, tm), col),
                 pl.BlockSpec((None, 2 * D_KV, tm), col), pl.BlockSpec((None, MISC_W, tm), col)]
    scratch = []
    if with_cmp:
        out_shape.append(jax.ShapeDtypeStruct((2 * N_KV, bsz * t // STRIDE, CMP_ROW), F32))
        out_specs.append(pl.BlockSpec((2 * N_KV, tm // STRIDE, CMP_ROW), lambda bi, i: (0, bi * nt + i, 0)))
        scratch.append(pltpu.VMEM((2 * D_KV // LANES, tm, LANES), F32))
    return pl.pallas_call(
        functools.partial(_proj_kernel, with_cmp=with_cmp), out_shape=tuple(out_shape), grid=(bsz, nt),
        in_specs=[pl.BlockSpec((None, tm, D_MODEL), row), _const_spec((1, D_MODEL)),
                  _const_spec((D_MODEL, _A_END)), _const_spec((_B_END, D_MODEL)),
                  _const_spec((1, MISC_W)), _const_spec((MISC_W, 1))],
        out_specs=tuple(out_specs), scratch_shapes=scratch,
        compiler_params=_cparams(("parallel", "parallel")), name="proj")(x, g, wa, wb, b, bt)


def _cumsum_kernel(lf_ref, tri_ref, c_ref):
    c_ref[...] = _dot3(lf_ref[...], tri_ref[...])


def _cumsum(misct):
    b, _, t = misct.shape
    tri = jnp.triu(jnp.ones((t, t), BF16))
    return pl.pallas_call(
        _cumsum_kernel, out_shape=jax.ShapeDtypeStruct((b, N_FOX, t), F32), grid=(b,),
        in_specs=[pl.BlockSpec((None, N_FOX, t), lambda i: (i, 0, 0)), _const_spec((t, t))],
        out_specs=pl.BlockSpec((None, N_FOX, t), lambda i: (i, 0, 0)),
        compiler_params=_cparams(("parallel",)), name="fox_cumsum")(misct, tri)


def _fox_kernel(q_ref, k_ref, v_ref, c_ref, o_ref, kb_sc, vb_sc, cb_sc, *, tq, nh):
    qi = pl.program_id(2)
    t = k_ref.shape[1]
    tk = tq
    nk = t // tk

    @pl.when(qi == 0)
    def _():
        for j in range(nk):
            cols = slice(j * tk, (j + 1) * tk)
            cb_sc[j] = c_ref[:, cols]
            for h in range(nh):
                rows = slice(h * HEAD_DIM, (h + 1) * HEAD_DIM)
                kb_sc[j, h] = k_ref[rows, cols].astype(BF16)
                vb_sc[j, h] = _aug_values(v_ref[rows, cols])

    q = q_ref[...]
    qs = [q[:, h * HEAD_DIM:(h + 1) * HEAD_DIM] for h in range(nh)]
    causal = (lax.broadcasted_iota(jnp.int32, (tq, tk), 1) <= lax.broadcasted_iota(jnp.int32, (tq, tk), 0))

    def step(kj, carry, diagonal):
        ck = cb_sc[kj]
        out = []
        for h in range(nh):
            s = _dot(qs[h], kb_sc[kj, h]) - ck[h:h + 1, :]
            if diagonal:
                s = jnp.where(causal, s, NEG)
            out.append(_flash_update(s, *carry[h], vb_sc[kj, h]))
        return tuple(out)

    init = tuple((jnp.full((tq, 1), NEG, F32), jnp.zeros((tq, V_AUG), F32)) for _ in range(nh))
    carry = lax.fori_loop(0, qi, functools.partial(step, diagonal=False), init)
    carry = step(qi, carry, True)
    o_ref[...] = jnp.concatenate([_flash_out(acc) for _, acc in carry], axis=1).astype(BF16)


def _fox_prompt(qf, fkvt, crow, tq=512, nh=4):
    b, t, _ = qf.shape
    tq = min(tq, t)
    nk = t // tq
    ng = N_FOX // nh
    w = nh * HEAD_DIM
    crow = crow.reshape(b, ng, nh, t)
    return pl.pallas_call(
        functools.partial(_fox_kernel, tq=tq, nh=nh),
        out_shape=jax.ShapeDtypeStruct((b, t, D_FOX), BF16),
        grid=(b, ng, t // tq),
        in_specs=[pl.BlockSpec((None, tq, w), lambda bi, hg, qi: (bi, qi, hg)),
                  pl.BlockSpec((None, w, t), lambda bi, hg, qi: (bi, hg, 0)),
                  pl.BlockSpec((None, w, t), lambda bi, hg, qi: (bi, ng + hg, 0)),
                  pl.BlockSpec((None, None, nh, t), lambda bi, hg, qi: (bi, hg, 0, 0))],
        out_specs=pl.BlockSpec((None, tq, w), lambda bi, hg, qi: (bi, qi, hg)),
        scratch_shapes=[pltpu.VMEM((nk, nh, HEAD_DIM, tq), BF16), pltpu.VMEM((nk, nh, V_AUG, tq), BF16),
                        pltpu.VMEM((nk, nh, tq), F32)],
        compiler_params=_cparams(("parallel", "parallel", "arbitrary")), name="fox_prompt")(qf, fkvt, fkvt, crow)


def _compress_rows(x, pe_ref, w1_ref, w2_ref):
    r = x.shape[0]
    xa = (x + pe_ref[0:1, :]).astype(BF16)
    xb = (x + pe_ref[1:2, :]).astype(BF16)
    ha = _dot(xa, w1_ref[:CMP_ROW, :])
    hb = _dot(xb, w1_ref[CMP_ROW:, :])
    hid = ha + pltpu.roll(hb, r - 1, 0)
    gl = 0.5 * hid * (1.0 + jnp.tanh(0.7978845608028654 * (hid + 0.044715 * (hid * hid * hid))))
    return _dot(gl.astype(BF16), w2_ref[...])


def _compress_kernel(x_ref, pe_ref, w1_ref, w2_ref, o_ref):
    o_ref[...] = _compress_rows(x_ref[...], pe_ref, w1_ref, w2_ref).astype(BF16)


def _compress_prompt(xcmp, pe2, w1, w2, b, t):
    r = t // STRIDE
    return pl.pallas_call(
        _compress_kernel,
        out_shape=jax.ShapeDtypeStruct((b, 2 * N_KV, r, HEAD_DIM), BF16),
        grid=(b, 2 * N_KV),
        in_specs=[pl.BlockSpec((None, r, CMP_ROW), lambda bi, j: (j, bi, 0)),
                  pl.BlockSpec((None, 2, CMP_ROW), lambda bi, j: (j // N_KV, 0, 0)),
                  pl.BlockSpec((None, L_CMP * HEAD_DIM, LANES), lambda bi, j: (j // N_KV, 0, 0)),
                  pl.BlockSpec((None, LANES, HEAD_DIM), lambda bi, j: (j // N_KV, 0, 0))],
        out_specs=pl.BlockSpec((None, None, r, HEAD_DIM), lambda bi, j: (bi, j, 0, 0)),
        compiler_params=_cparams(("parallel", "parallel")), name="nsa_compress")(xcmp, pe2, w1, w2)


def _slope_col(slopes_ref, g, hrow):
    s0, s1, s2, s3 = (slopes_ref[g, h] for h in range(HPG))
    return jnp.where(hrow == 0, s0, jnp.where(hrow == 1, s1, jnp.where(hrow == 2, s2, s3)))


def _select_blocks(p_slc, cur, nsb, axis):
    jb = lax.broadcasted_iota(jnp.int32, p_slc.shape, axis)
    valid = jb <= cur
    forced = valid & ((jb == 0) | ((cur - jb) < N_LOCAL))
    score = jnp.where(valid, p_slc, -jnp.inf)
    score = jnp.where(forced, jnp.inf, score)
    rank = jnp.zeros(p_slc.shape, jnp.int32)
    for i in range(nsb):
        ci = score[:, i:i + 1] if axis == 1 else score[i:i + 1, :]
        beats = (ci > score) | ((ci == score) & (jb > i))
        rank = rank + jnp.where(beats, 1, 0)
    return jnp.where((rank < N_SEL) & (jb < nsb), 1.0, 0.0)


def _cmp_branch(q4, kc, vc, slope, qpos, tq):
    ncp = kc.shape[0]
    s = _dot_nt(q4, kc)
    cend = lax.broadcasted_iota(jnp.int32, (1, ncp), 1) * STRIDE + (L_CMP - 1)
    s = s - slope * (qpos.astype(F32) - cend.astype(F32))
    pc = _masked_softmax(s, cend <= qpos)
    o_c = _dot(pc.astype(BF16), vc)
    return o_c, pc[0:tq] + pc[tq:2 * tq] + pc[2 * tq:3 * tq] + pc[3 * tq:4 * tq]


def _gate_col(misc, br, g):
    cols = [_lane_pick(misc, GATE_OFF + br * N_NSA + g * HPG + h) for h in range(HPG)]
    return jnp.concatenate(cols, axis=0)


def _slc_matrix(nc, nsb, ncp, width):
    r, c = L_SEL // STRIDE, L_CMP // STRIDE
    offs, counts = np.unique((np.arange(r)[:, None] - np.arange(c)[None, :]).ravel(), return_counts=True)
    m = np.zeros((ncp, width), np.float32)
    for j in range(nsb):
        for off, cnt in zip(offs, counts):
            n = r * j + off
            if 0 <= n < nc:
                m[n, j] += cnt
    return jnp.asarray(m, BF16)


def _alibi_slopes():
    base = 2.0 ** (-8.0 * np.arange(1, N_NSA + 1, dtype=np.float32) / N_NSA)
    return jnp.asarray(base.astype(np.float32).reshape(HPG, N_KV).T)


def _nsa_kernel(slopes_ref, q_ref, misc_ref, cmp_ref, ks_ref, vs_ref, kw_ref, vw_ref, m_ref, e_ref, o_ref,
                ksb, vsb, kwb, vwb, *, tq, tk, nsb):
    g = pl.program_id(1)
    qi = pl.program_id(2)
    t = ks_ref.shape[1]
    nk = t // tk
    wch = LANES
    nwc = t // wch

    @pl.when(qi == 0)
    def _():
        for j in range(nk):
            ksb[j] = ks_ref[:, j * tk:(j + 1) * tk].astype(BF16)
            vsb[j] = _aug_values(vs_ref[:, j * tk:(j + 1) * tk])
        for j in range(nwc):
            kwb[j] = kw_ref[:, j * wch:(j + 1) * wch].astype(BF16)
            vwb[j] = _aug_values(vw_ref[:, j * wch:(j + 1) * wch])

    a = qi * tq
    rows = HPG * tq
    row = lax.broadcasted_iota(jnp.int32, (rows, 1), 0)
    qpos = a + row % tq
    hrow = row // tq
    qpos1 = a + lax.broadcasted_iota(jnp.int32, (tq, 1), 0)
    misc = misc_ref[...]
    slope = _slope_col(slopes_ref, g, hrow)
    q4 = jnp.concatenate([q_ref[:, h * HEAD_DIM:(h + 1) * HEAD_DIM] for h in range(HPG)], axis=0)

    o_c, pg = _cmp_branch(q4, cmp_ref[g], cmp_ref[N_KV + g], slope, qpos, tq)
    hi, mid, lo = _split3(pg)
    mt = m_ref[...]
    p_slct = _dot_nt(mt, hi) + _dot_nt(mt, mid) + _dot_nt(mt, lo)
    cur = (a + lax.broadcasted_iota(jnp.int32, (1, tq), 1)) // L_SEL
    selt = _select_blocks(p_slct, cur, nsb, 0)
    sel = jnp.concatenate([selt, jnp.zeros((LANES - selt.shape[0], tq), F32)], axis=0).T.astype(BF16)

    def step(kj, carry, diagonal):
        kpos = kj * tk + lax.broadcasted_iota(jnp.int32, (1, tk), 1)
        keep = _dot(sel, e_ref[kj]) > 0.5
        if diagonal:
            keep = keep & (kpos <= qpos1)
        s4 = _dot(q4, ksb[kj])
        drel = (kpos - a).astype(F32)
        out = []
        for h in range(HPG):
            s = s4[h * tq:(h + 1) * tq] + slopes_ref[g, h] * drel
            out.append(_flash_update(jnp.where(keep, s, NEG), *carry[h], vsb[kj]))
        return tuple(out)

    init = tuple((jnp.full((tq, 1), NEG, F32), jnp.zeros((tq, V_AUG), F32)) for _ in range(HPG))
    kd = a // tk
    carry = lax.fori_loop(0, kd, functools.partial(step, diagonal=False), init)
    carry = step(kd, carry, True)

    full = WINDOW // wch + 1
    nwin = min(full, nwc)
    c0 = qi + 1 - nwin
    rr = lax.broadcasted_iota(jnp.int32, (tq, wch), 0)
    cc = lax.broadcasted_iota(jnp.int32, (tq, wch), 1)
    s_h = [[] for _ in range(HPG)]
    cis = []
    for i in range(nwin):
        exists = c0 + i >= 0
        ci = jnp.maximum(c0 + i, 0)
        cis.append(ci)
        s4 = _dot(q4, kwb[ci])
        drel = ((c0 + i) * wch - a + lax.broadcasted_iota(jnp.int32, (1, wch), 1)).astype(F32)
        if i == nwin - 1:
            keep = cc <= rr
        elif i == 0 and nwin == full:
            keep = (cc > rr) & exists
        else:
            keep = exists
        for h in range(HPG):
            s_h[h].append(jnp.where(keep, s4[h * tq:(h + 1) * tq] + slopes_ref[g, h] * drel, NEG))
    gate = lambda br, h: _lane_pick(misc, GATE_OFF + br * N_NSA + g * HPG + h)
    outs = []
    for h in range(HPG):
        m = jnp.max(functools.reduce(jnp.maximum, s_h[h]), axis=1, keepdims=True)
        acc = _dot_nt(jnp.exp(s_h[h][0] - m).astype(BF16), vwb[cis[0]])
        for i in range(1, nwin):
            acc = acc + _dot_nt(jnp.exp(s_h[h][i] - m).astype(BF16), vwb[cis[i]])
        outs.append(o_c[h * tq:(h + 1) * tq] * gate(0, h) + _flash_out(carry[h][1], gate(1, h))
                    + _flash_out(acc, gate(2, h)))
    o_ref[...] = jnp.concatenate(outs, axis=1).astype(BF16)


def _nsa_prompt(qn, misc, cmp, nkvt, wint, tq=128, tk=512):
    b, t, _ = qn.shape
    tq = min(tq, t)
    tk = min(tk, t)
    assert tk % tq == 0 and tq == LANES
    ncp = t // STRIDE
    nc = (t - L_CMP) // STRIDE + 1
    nsb = -(-t // L_SEL)
    assert nsb <= LANES
    nsbp = -(-nsb // SUBLANES) * SUBLANES
    m = _slc_matrix(nc, nsb, ncp, nsbp).T
    e = (np.arange(t)[None, :] // L_SEL == np.arange(LANES)[:, None]).astype(np.float32)
    e = jnp.asarray(e.reshape(LANES, t // tk, tk).transpose(1, 0, 2), BF16)
    hw = HPG * HEAD_DIM
    grid_spec = pl.GridSpec(
        grid=(b, N_KV, t // tq),
        in_specs=[pl.BlockSpec(memory_space=pltpu.SMEM),
                  pl.BlockSpec((None, tq, hw), lambda bi, g, qi: (bi, qi, g)),
                  pl.BlockSpec((None, tq, MISC_W), lambda bi, g, qi: (bi, qi, 0)),
                  pl.BlockSpec((None, 2 * N_KV, ncp, HEAD_DIM), lambda bi, g, qi: (bi, 0, 0, 0)),
                  pl.BlockSpec((None, HEAD_DIM, t), lambda bi, g, qi: (bi, 2 * N_KV + g, 0)),
                  pl.BlockSpec((None, HEAD_DIM, t), lambda bi, g, qi: (bi, 3 * N_KV + g, 0)),
                  pl.BlockSpec((None, HEAD_DIM, t), lambda bi, g, qi: (bi, g, 0)),
                  pl.BlockSpec((None, HEAD_DIM, t), lambda bi, g, qi: (bi, N_KV + g, 0)),
                  _const_spec((nsbp, ncp)), _const_spec((t // tk, LANES, tk))],
        out_specs=pl.BlockSpec((None, tq, hw), lambda bi, g, qi: (bi, qi, g)),
        scratch_shapes=[pltpu.VMEM((t // tk, HEAD_DIM, tk), BF16), pltpu.VMEM((t // tk, V_AUG, tk), BF16),
                        pltpu.VMEM((t // LANES, HEAD_DIM, LANES), BF16),
                        pltpu.VMEM((t // LANES, V_AUG, LANES), BF16)])
    return pl.pallas_call(
        functools.partial(_nsa_kernel, tq=tq, tk=tk, nsb=nsb),
        out_shape=jax.ShapeDtypeStruct((b, t, D_NSA), BF16), grid_spec=grid_spec,
        compiler_params=_cparams(("parallel", "parallel", "arbitrary")), name="nsa_prompt",
    )(_alibi_slopes(), qn, misc, cmp, nkvt, nkvt, wint, wint, m, e)


def _fox_dec_kernel(*refs, n_new, ppb):
    pt_ref, q_ref = refs[:2]
    kv_refs = refs[2:2 + ppb]
    lf_refs = refs[2 + ppb:2 + 2 * ppb]
    kvn_ref, lfn_ref, gt_ref, le_ref, o_ref, qbd_sc, m_sc, l_sc, r_sc, acc_sc = refs[2 + 2 * ppb:]
    p = pl.program_id(1)
    page = kvn_ref.shape[2]
    rows = n_new * N_FOX

    def tile_heads(x):
        return jnp.concatenate([x] * n_new, axis=0)

    def accumulate(kt, vt, bias, mask):
        s = _dot(qbd_sc[...], kt) + bias
        if mask is not None:
            s = jnp.where(mask, s, NEG)
        m, alpha, l, pr = _online_softmax_step(s, m_sc[...], l_sc[...])
        m_sc[...] = m
        l_sc[...] = l
        acc_sc[...] = alpha * acc_sc[...] + _dot_nt(pr.astype(BF16), vt)

    @pl.when(p == 0)
    def _():
        q = q_ref[...]
        rep = jnp.concatenate([jnp.broadcast_to(q[t:t + 1, :], (N_FOX, D_FOX)) for t in range(n_new)], axis=0)
        rr = lax.broadcasted_iota(jnp.int32, (rows, D_FOX), 0)
        ll = lax.broadcasted_iota(jnp.int32, (rows, D_FOX), 1)
        qbd_sc[...] = jnp.where(ll // HEAD_DIM == rr % N_FOX, rep, 0.0).astype(BF16)
        m_sc[...] = jnp.full(m_sc.shape, NEG, F32)
        l_sc[...] = jnp.zeros(l_sc.shape, F32)
        r_sc[...] = jnp.zeros(r_sc.shape, F32)
        acc_sc[...] = jnp.zeros(acc_sc.shape, F32)
        cn = _dot3(lfn_ref[...], le_ref[...])
        tkey = lax.broadcasted_iota(jnp.int32, (rows, page), 1)
        tqry = lax.broadcasted_iota(jnp.int32, (rows, page), 0) // N_FOX
        accumulate(kvn_ref[0].astype(BF16), kvn_ref[1].astype(BF16), -tile_heads(cn),
                   (tkey <= tqry) & (tkey < n_new))

    later = r_sc[...]
    suffix = []
    for lf_ref in lf_refs:
        lf = lf_ref[...]
        suffix.append(_dot3(lf, gt_ref[...]) + later)
        later = later + jnp.sum(lf, axis=1, keepdims=True)
    r_sc[...] = later
    accumulate(jnp.concatenate([r[0].astype(BF16) for r in kv_refs], axis=1),
               jnp.concatenate([r[1].astype(BF16) for r in kv_refs], axis=1),
               tile_heads(jnp.concatenate(suffix, axis=1)), None)

    @pl.when(p == pl.num_programs(1) - 1)
    def _():
        o = acc_sc[...] / jnp.maximum(l_sc[...], 1e-30)
        rr = lax.broadcasted_iota(jnp.int32, o.shape, 0)
        ll = lax.broadcasted_iota(jnp.int32, o.shape, 1)
        o = jnp.where(ll // HEAD_DIM == rr % N_FOX, o, 0.0)
        orow = lax.broadcasted_iota(jnp.int32, (SQ, D_FOX), 0)
        out = jnp.zeros((SQ, D_FOX), F32)
        for t in range(n_new):
            out = jnp.where(orow == t, jnp.sum(o[t * N_FOX:(t + 1) * N_FOX], axis=0, keepdims=True), out)
        o_ref[...] = out


def _fox_decode(pt_flat, qf_pad, cache_kvt, cache_lft, kvt_new, lft_new, n_pages, n_new):
    db = qf_pad.shape[0]
    page = cache_kvt.shape[3]
    gt = np.tril(np.ones((page, page), np.float32), -1)
    le = np.triu(np.ones((page, page), np.float32))
    ppb = next(c for c in (8, 4, 2, 1) if n_pages % c == 0)
    pidx = lambda i: (lambda b, p, pt: (pt[b * n_pages + n_pages - 1 - (p * ppb + i)], 0, 0, 0))
    pidx3 = lambda i: (lambda b, p, pt: (pt[b * n_pages + n_pages - 1 - (p * ppb + i)], 0, 0))
    rows = n_new * N_FOX
    grid_spec = pltpu.PrefetchScalarGridSpec(
        num_scalar_prefetch=1, grid=(db, n_pages // ppb),
        in_specs=[pl.BlockSpec((None, SQ, D_FOX), lambda b, p, pt: (b, 0, 0))]
        + [pl.BlockSpec((None, 2, D_FOX, page), pidx(i)) for i in range(ppb)]
        + [pl.BlockSpec((None, N_FOX, page), pidx3(i)) for i in range(ppb)]
        + [pl.BlockSpec((None, 2, D_FOX, page), lambda b, p, pt: (b, 0, 0, 0)),
           pl.BlockSpec((None, N_FOX, page), lambda b, p, pt: (b, 0, 0)),
           _const_spec((page, page)), _const_spec((page, page))],
        out_specs=pl.BlockSpec((None, SQ, D_FOX), lambda b, p, pt: (b, 0, 0)),
        scratch_shapes=[pltpu.VMEM((rows, D_FOX), BF16), pltpu.VMEM((rows, 1), F32), pltpu.VMEM((rows, 1), F32),
                        pltpu.VMEM((N_FOX, 1), F32), pltpu.VMEM((rows, D_FOX), F32)])
    return pl.pallas_call(
        functools.partial(_fox_dec_kernel, n_new=n_new, ppb=ppb),
        out_shape=jax.ShapeDtypeStruct((db, SQ, D_FOX), F32), grid_spec=grid_spec,
        compiler_params=_cparams(("parallel", "arbitrary")), name="fox_decode",
    )(pt_flat, qf_pad, *([cache_kvt] * ppb), *([cache_lft] * ppb), kvt_new, lft_new,
      jnp.asarray(gt, BF16), jnp.asarray(le, BF16))


def _nsa_pages_kernel(*refs, ppb):
    pg_refs = refs[1:1 + ppb]
    new_ref, pe_ref, w1_ref, w2_ref, ksvs_ref, cmp_ref, tr_sc, x_sc = refs[1 + ppb:]
    p = pl.program_id(1)
    npg = pl.num_programs(1) - 1
    page = new_ref.shape[1]
    rpp = page // STRIDE
    nch = D_KV // LANES

    @pl.when(p < npg)
    def _():
        ksvs_ref[...] = jnp.concatenate(
            [jnp.concatenate([r[2], r[3]], axis=0) for r in pg_refs], axis=1).astype(BF16)
        for k, pg_ref in enumerate(pg_refs):
            for s in range(2):
                for c in range(nch):
                    tr_sc[(k * 2 + s) * nch + c] = pg_ref[s, c * LANES:(c + 1) * LANES, :].T
            r0 = pl.multiple_of((p * ppb + k) * rpp, rpp)
            for l in range(STRIDE):
                for c in range(2 * nch):
                    rows = tr_sc[k * 2 * nch + c, pl.ds(l, rpp, stride=STRIDE), :]
                    for i in range(2):
                        x_sc[2 * c + i, pl.ds(r0, rpp), l * HEAD_DIM:(l + 1) * HEAD_DIM] = (
                            rows[:, i * HEAD_DIM:(i + 1) * HEAD_DIM])

    @pl.when(p == npg)
    def _():
        pad = [jnp.zeros((2 * D_KV, (ppb - 1) * page), F32)] if ppb > 1 else []
        ksvs_ref[...] = jnp.concatenate([new_ref[...]] + pad, axis=1).astype(BF16)
        for j in range(2 * N_KV):
            s = j // N_KV
            cmp_ref[j] = _compress_rows(x_sc[j], pe_ref.at[s], w1_ref.at[s], w2_ref.at[s]).astype(BF16)


def _nsa_pages(pt_flat, cache_nsat, ksvst_new, pe2, w1, w2, n_pages):
    db = ksvst_new.shape[0]
    page = cache_nsat.shape[3]
    past = n_pages * page
    rows = past // STRIDE
    ppb = next(c for c in (8, 4, 2, 1) if n_pages % c == 0)
    nsteps = n_pages // ppb
    pidx = lambda i: (lambda b, p, pt: (pt[b * n_pages + jnp.minimum(p * ppb + i, n_pages - 1)], 0, 0, 0))
    grid_spec = pltpu.PrefetchScalarGridSpec(
        num_scalar_prefetch=1, grid=(db, nsteps + 1),
        in_specs=[pl.BlockSpec((None, 4, D_KV, page), pidx(i)) for i in range(ppb)]
        + [pl.BlockSpec((None, 2 * D_KV, page), lambda b, p, pt: (b, 0, 0)),
           _const_spec((2, 2, CMP_ROW)), _const_spec((2, L_CMP * HEAD_DIM, LANES)),
           _const_spec((2, LANES, HEAD_DIM))],
        out_specs=(pl.BlockSpec((None, 2 * D_KV, ppb * page), lambda b, p, pt: (b, 0, p)),
                   pl.BlockSpec((None, 2 * N_KV, rows, HEAD_DIM), lambda b, p, pt: (b, 0, 0, 0))),
        scratch_shapes=[pltpu.VMEM((ppb * 2 * D_KV // LANES, page, LANES), F32),
                        pltpu.VMEM((2 * N_KV, rows, CMP_ROW), F32)])
    return pl.pallas_call(
        functools.partial(_nsa_pages_kernel, ppb=ppb),
        out_shape=(jax.ShapeDtypeStruct((db, 2 * D_KV, past + ppb * page), BF16),
                   jax.ShapeDtypeStruct((db, 2 * N_KV, rows, HEAD_DIM), BF16)),
        grid_spec=grid_spec, compiler_params=_cparams(("parallel", "arbitrary")), name="nsa_pages",
    )(pt_flat, *([cache_nsat] * ppb), ksvst_new, pe2, w1, w2)


def _nsa_dec_kernel(slopes_ref, q_ref, misc_ref, cmp_ref, ksvs_ref, win_ref, winn_ref, m_ref, e_ref, o_ref,
                    selx_sc, *, past, nsb):
    tq = SQ
    rows = HPG * tq
    row = lax.broadcasted_iota(jnp.int32, (rows, 1), 0)
    qpos = past + row % tq
    qposf = qpos.astype(F32)
    hrow = row // tq
    misc = misc_ref[...]
    q = q_ref[...]
    cur = (past + lax.broadcasted_iota(jnp.int32, (tq, 1), 0)) // L_SEL
    nkeys = ksvs_ref.shape[1]

    o_cs, sels, q4s, slopes = [], [], [], []
    for g in range(N_KV):
        slope = _slope_col(slopes_ref, g, hrow)
        q4 = jnp.concatenate(
            [q[:, (g * HPG + h) * HEAD_DIM:(g * HPG + h + 1) * HEAD_DIM] for h in range(HPG)], axis=0).astype(BF16)
        o_c, pg = _cmp_branch(q4, cmp_ref[g], cmp_ref[N_KV + g], slope, qpos, tq)
        sels.append(_select_blocks(_dot3(pg, m_ref[...]), cur, nsb, 1))
        o_cs.append(o_c)
        q4s.append(q4)
        slopes.append(slope)
    selx_sc[...] = _dot(jnp.concatenate(sels, axis=0).astype(BF16), e_ref[...])

    kpos = lax.broadcasted_iota(jnp.int32, (1, nkeys), 1)
    nwin = win_ref.shape[2]
    wpos = (past - nwin) + lax.broadcasted_iota(jnp.int32, (1, nwin + winn_ref.shape[2]), 1)
    outs = []
    for g in range(N_KV):
        q4, slope = q4s[g], slopes[g]
        gr = slice(g * HEAD_DIM, (g + 1) * HEAD_DIM)
        s = _dot(q4, ksvs_ref[gr, :]) - slope * (qposf - kpos.astype(F32))
        selx = selx_sc[g * tq:(g + 1) * tq, :]
        mask = (jnp.concatenate([selx] * HPG, axis=0) > 0.5) & (kpos <= qpos)
        o_s = _dot_nt(_masked_softmax(s, mask).astype(BF16), ksvs_ref[D_KV + g * HEAD_DIM:D_KV + (g + 1) * HEAD_DIM, :])
        kw = jnp.concatenate([win_ref[0, gr, :], winn_ref[0, gr, :]], axis=1).astype(BF16)
        vw = jnp.concatenate([win_ref[1, gr, :], winn_ref[1, gr, :]], axis=1).astype(BF16)
        dist = qpos - wpos
        s = _dot(q4, kw) - slope * dist.astype(F32)
        o_w = _dot_nt(_masked_softmax(s, (dist >= 0) & (dist < WINDOW)).astype(BF16), vw)
        o = _gate_col(misc, 0, g) * o_cs[g] + _gate_col(misc, 1, g) * o_s + _gate_col(misc, 2, g) * o_w
        outs += [o[h * tq:(h + 1) * tq] for h in range(HPG)]
    o_ref[...] = jnp.concatenate(outs, axis=1)


def _nsa_decode(qn_pad, misc_pad, cmp, ksvst, wint_state, wint_new, past, n_new):
    db = qn_pad.shape[0]
    nkeys = ksvst.shape[2]
    ncp = cmp.shape[2]
    tk_all = past + n_new
    nc = (tk_all - L_CMP) // STRIDE + 1
    nsb = -(-tk_all // L_SEL)
    width = -(-nsb // LANES) * LANES
    m = _slc_matrix(nc, nsb, ncp, width)
    e = jnp.asarray(np.arange(nkeys)[None, :] // L_SEL == np.arange(width)[:, None], BF16)
    nwin = wint_state.shape[3]
    nwn = wint_new.shape[3]
    per_b3 = lambda b: (b, 0, 0)
    per_b4 = lambda b: (b, 0, 0, 0)
    grid_spec = pl.GridSpec(
        grid=(db,),
        in_specs=[pl.BlockSpec(memory_space=pltpu.SMEM),
                  pl.BlockSpec((None, SQ, D_NSA), per_b3), pl.BlockSpec((None, SQ, MISC_W), per_b3),
                  pl.BlockSpec((None, 2 * N_KV, ncp, HEAD_DIM), per_b4),
                  pl.BlockSpec((None, 2 * D_KV, nkeys), per_b3),
                  pl.BlockSpec((None, 2, D_KV, nwin), per_b4), pl.BlockSpec((None, 2, D_KV, nwn), per_b4),
                  _const_spec((ncp, width)), _const_spec((width, nkeys))],
        out_specs=pl.BlockSpec((None, SQ, D_NSA), per_b3),
        scratch_shapes=[pltpu.VMEM((N_KV * SQ, nkeys), F32)])
    return pl.pallas_call(
        functools.partial(_nsa_dec_kernel, past=past, nsb=nsb),
        out_shape=jax.ShapeDtypeStruct((db, SQ, D_NSA), F32), grid_spec=grid_spec,
        compiler_params=_cparams(("parallel",)), name="nsa_decode",
    )(_alibi_slopes(), qn_pad, misc_pad, cmp, ksvst, wint_state, wint_new, m, e)


def _prep_w_in(w_in, b_forget, b_gate):
    sizes = (D_FOX, D_FOX, D_FOX, N_FOX, D_NSA) + (D_KV,) * 6 + (3 * N_NSA,)
    offs = np.cumsum((0,) + sizes)
    part = lambda i, j=None: w_in[:, offs[i]:offs[(i if j is None else j) + 1]]
    pad = MISC_W - N_FOX - 3 * N_NSA
    misc = jnp.concatenate([part(3), part(11), jnp.zeros((w_in.shape[0], pad), w_in.dtype)], axis=1)
    wa = jnp.concatenate([part(0), part(4), part(5, 6), misc], axis=1).astype(BF16)
    wb = jnp.concatenate([part(1, 2), part(5, 10), misc], axis=1).T.astype(BF16)
    b = jnp.concatenate([b_forget, b_gate, jnp.zeros((pad,), F32)]).astype(F32)
    return wa, wb, b.reshape(1, MISC_W), b.reshape(MISC_W, 1)


def _pos_minor(x):
    nd = x.ndim
    return jnp.transpose(x, tuple(range(nd - 4)) + (nd - 3, nd - 2, nd - 1, nd - 4))


def _pos_major(x):
    nd = x.ndim
    return jnp.transpose(x, tuple(range(nd - 4)) + (nd - 1, nd - 4, nd - 3, nd - 2))


def kernel(x_prompt, x_sample, cache_fox_kv, cache_fox_logf, cache_nsa_kv, state_win_kv, page_table, norm_ffn1,
           ffn1_gate, ffn1_up, ffn1_down, norm_mix, w_in, b_forget, b_gate, phi_pe, phi_w1, phi_w2, w_out,
           norm_ffn2, ffn2_gate, ffn2_up, ffn2_down, norm_final):
    depth = norm_ffn1.shape[0]
    bsz, t, _ = x_prompt.shape
    db, s_new, _ = x_sample.shape
    n_pool, page = cache_fox_kv.shape[1], cache_fox_kv.shape[2]
    n_pages = page_table.shape[1]
    past = n_pages * page
    wb_len = state_win_kv.shape[2]
    wp = min(WINDOW, t)
    assert s_new <= SQ and wb_len == min(WINDOW, past)

    xp = x_prompt
    xs = x_sample.reshape(1, db * s_new, D_MODEL)
    fox_kvt_pool = _pos_minor(cache_fox_kv).reshape(depth * n_pool, 2, D_FOX, page)
    fox_lft_pool = jnp.swapaxes(cache_fox_logf, 2, 3).reshape(depth * n_pool, N_FOX, page)
    nsa_kvt_pool = _pos_minor(cache_nsa_kv).reshape(depth * n_pool, 4, D_KV, page)
    win_t = _pos_minor(state_win_kv).reshape(depth, db, 2, D_KV, wb_len)
    gfin = norm_final.reshape(1, D_MODEL)
    outs = [[] for _ in range(8)]

    def pad_rows(a):
        a = a.reshape(db, s_new, -1).astype(F32)
        return jnp.pad(a, ((0, 0), (0, SQ - s_new), (0, 0)))

    def new_cols(a_t, lead):
        a_t = jnp.transpose(a_t.reshape(a_t.shape[1], db, s_new), (1, 0, 2))
        return jnp.pad(a_t, ((0, 0), (0, 0), (0, page - s_new))).reshape((db,) + lead + (page,))

    for l in range(depth):
        last = l == depth - 1
        g1 = norm_ffn1[l].reshape(1, D_MODEL)
        gm = norm_mix[l].reshape(1, D_MODEL)
        g2 = norm_ffn2[l].reshape(1, D_MODEL)
        ffn1 = (ffn1_gate[l].astype(BF16), ffn1_up[l].astype(BF16), ffn1_down[l].astype(BF16))
        ffn2 = (ffn2_gate[l].astype(BF16), ffn2_up[l].astype(BF16), ffn2_down[l].astype(BF16))
        wa, wbt, b_r, b_c = _prep_w_in(w_in[l], b_forget[l], b_gate[l])
        wo = w_out[l].astype(BF16)
        pe2 = phi_pe[l].reshape(2, 2, CMP_ROW)
        w1 = phi_w1[l].astype(BF16)
        w2 = phi_w2[l].astype(BF16)
        pt_flat = (page_table + l * n_pool).reshape(-1).astype(jnp.int32)

        xp = _ffn(xp.reshape(bsz * t, D_MODEL), g1, *ffn1).reshape(bsz, t, D_MODEL)
        xs = _ffn(xs.reshape(db * s_new, D_MODEL), g1, *ffn1).reshape(1, db * s_new, D_MODEL)

        qf, qn, misc, fkvt, nkvt, wint, misct, xcmp = _proj(xp, gm, wa, wbt, b_r, b_c, True)
        o_f = _fox_prompt(qf, fkvt, _cumsum(misct))
        cmp = _compress_prompt(xcmp, pe2, w1, w2, bsz, t)
        o_n = _nsa_prompt(qn, misc, cmp, nkvt, wint)
        xp = _ffn(xp.reshape(bsz * t, D_MODEL), g2, *ffn2,
                  mix=(o_f.reshape(bsz * t, D_FOX), o_n.reshape(bsz * t, D_NSA), wo),
                  gf=gfin if last else None).reshape(bsz, t, D_MODEL)
        outs[0].append(_pos_major(fkvt.reshape(bsz, 2, N_FOX, HEAD_DIM, t)))
        outs[1].append(jnp.swapaxes(misct[:, :N_FOX, :], 1, 2))
        outs[2].append(_pos_major(nkvt.reshape(bsz, 4, N_KV, HEAD_DIM, t)))
        outs[3].append(_pos_major(wint[:, :, t - wp:].reshape(bsz, 2, N_KV, HEAD_DIM, wp)))

        qf, qn, misc, fkvt, nkvt, wint, misct = _proj(xs, gm, wa, wbt, b_r, b_c, False)
        o_f = _fox_decode(pt_flat, pad_rows(qf), fox_kvt_pool, fox_lft_pool, new_cols(fkvt, (2, D_FOX)),
                          new_cols(misct[:, :N_FOX, :], (N_FOX,)), n_pages, s_new)
        ksvst, cmp = _nsa_pages(pt_flat, nsa_kvt_pool, new_cols(nkvt[:, 2 * D_KV:, :], (2 * D_KV,)),
                                pe2, w1, w2, n_pages)
        o_n = _nsa_decode(pad_rows(qn), pad_rows(misc), cmp, ksvst, win_t[l], new_cols(wint, (2, D_KV)),
                          past, s_new)
        o_f = o_f[:, :s_new].reshape(db * s_new, D_FOX).astype(BF16)
        o_n = o_n[:, :s_new].reshape(db * s_new, D_NSA).astype(BF16)
        xs = _ffn(xs.reshape(db * s_new, D_MODEL), g2, *ffn2, mix=(o_f, o_n, wo),
                  gf=gfin if last else None).reshape(1, db * s_new, D_MODEL)
        tok_major = lambda a_t, lead: jnp.transpose(a_t[0], (1, 0)).reshape((db, s_new) + lead)
        win_new = tok_major(wint, (2, N_KV, HEAD_DIM))
        outs[4].append(tok_major(fkvt, (2, N_FOX, HEAD_DIM)))
        outs[5].append(tok_major(misct[:, :N_FOX, :], (N_FOX,)))
        outs[6].append(tok_major(nkvt, (4, N_KV, HEAD_DIM)))
        outs[7].append(jnp.concatenate([state_win_kv[l], win_new], axis=1)[:, -wb_len:])

    y_prompt = xp
    y_sample = xs.reshape(db, s_new, D_MODEL)
    return (y_prompt, y_sample) + tuple(jnp.stack(o) for o in outs)
```

```python
import functools

import numpy as np
import jax
import jax.numpy as jnp
from jax import lax
from jax.experimental import pallas as pl
from jax.experimental.pallas import tpu as pltpu

F32 = jnp.float32
BF16 = jnp.bfloat16

D_MODEL = 2048
HEAD_DIM = 64
N_FOX = 16
N_NSA = 16
N_KV = 4
HPG = N_NSA // N_KV
D_FOX = N_FOX * HEAD_DIM
D_NSA = N_NSA * HEAD_DIM
D_KV = N_KV * HEAD_DIM
D_FF = 5632
L_CMP = 32
STRIDE = 16
L_SEL = 64
N_SEL = 16
N_LOCAL = 2
WINDOW = 512
RMS_EPS = 1e-6
NEG = -1e30
SCALE = HEAD_DIM ** -0.5

LANES = 128
SUBLANES = 8
MISC_W = LANES
GATE_OFF = N_FOX
SQ = SUBLANES
CMP_ROW = STRIDE * HEAD_DIM
V_AUG = HEAD_DIM + 2 * SUBLANES

_A_QF = 0
_A_QN = _A_QF + D_FOX
_A_KCVC = _A_QN + D_NSA
_A_MISC = _A_KCVC + 2 * D_KV
_A_END = _A_MISC + MISC_W
_B_FKV = 0
_B_NKV = _B_FKV + 2 * D_FOX
_B_WIN = _B_NKV + 4 * D_KV
_B_MISC = _B_WIN + 2 * D_KV
_B_END = _B_MISC + MISC_W

_VMEM_LIMIT = 56 * 1024 * 1024


def _cparams(sem):
    return pltpu.CompilerParams(dimension_semantics=sem, vmem_limit_bytes=_VMEM_LIMIT)


def _const_spec(shape):
    nd = len(shape)
    return pl.BlockSpec(shape, lambda *_: (0,) * nd, pipeline_mode=pl.Buffered(1))


def _dot(a, b):
    return jnp.dot(a, b, preferred_element_type=F32)


def _dot_nt(a, b):
    return lax.dot_general(a, b, (((1,), (1,)), ((), ())), preferred_element_type=F32)


def _split3(x):
    hi = x.astype(BF16)
    r1 = x - hi.astype(F32)
    mid = r1.astype(BF16)
    lo = (r1 - mid.astype(F32)).astype(BF16)
    return hi, mid, lo


def _dot3(x, m):
    hi, mid, lo = _split3(x)
    return _dot(hi, m) + _dot(mid, m) + _dot(lo, m)


def _rms(x, g):
    return x * lax.rsqrt(jnp.mean(x * x, axis=-1, keepdims=True) + RMS_EPS) * g


def _masked_softmax(s, mask):
    s = jnp.where(mask, s, NEG)
    e = jnp.where(mask, jnp.exp(s - jnp.max(s, axis=-1, keepdims=True)), 0.0)
    return e / jnp.maximum(jnp.sum(e, axis=-1, keepdims=True), 1e-30)


def _online_softmax_step(s, m, l):
    m_new = jnp.maximum(m, jnp.max(s, axis=1, keepdims=True))
    p = jnp.exp(s - m_new)
    alpha = jnp.exp(m - m_new)
    return m_new, alpha, alpha * l + jnp.sum(p, axis=1, keepdims=True), p


def _aug_values(vt):
    extra = lax.broadcasted_iota(jnp.int32, (V_AUG - HEAD_DIM, vt.shape[1]), 0) == 0
    return jnp.concatenate([vt, jnp.where(extra, 1.0, 0.0)], axis=0).astype(BF16)


def _flash_update(s, m, acc, vt_aug):
    m_new = jnp.maximum(m, jnp.max(s, axis=1, keepdims=True))
    p = jnp.exp(s - m_new)
    return m_new, jnp.exp(m - m_new) * acc + _dot_nt(p.astype(BF16), vt_aug)


def _flash_out(acc, gate=1.0):
    return acc[:, :HEAD_DIM] * (gate / jnp.maximum(acc[:, HEAD_DIM:HEAD_DIM + 1], 1e-30))


def _lane_pick(x, idx):
    lane = lax.broadcasted_iota(jnp.int32, x.shape, 1)
    return jnp.sum(jnp.where(lane == idx, x, 0.0), axis=1, keepdims=True)


def _log_sigmoid(z):
    return jnp.minimum(z, 0.0) - jnp.log1p(jnp.exp(-jnp.abs(z)))


def _sigmoid(z):
    return 1.0 / (1.0 + jnp.exp(-z))


def _ffn_kernel(*refs, has_mix, final):
    it = iter(refs)
    x_ref, g_ref, wg_ref, wu_ref, wd_ref = (next(it) for _ in range(5))
    if has_mix:
        of_ref, on_ref, wo_ref = (next(it) for _ in range(3))
    if final:
        gf_ref = next(it)
    o_ref, h_sc, acc_sc = (next(it) for _ in range(3))
    j = pl.program_id(1)

    @pl.when(j == 0)
    def _():
        x = x_ref[...]
        if has_mix:
            x = x + _dot(of_ref[...], wo_ref[:D_FOX, :]) + _dot(on_ref[...], wo_ref[D_FOX:, :])
            o_ref[...] = x
        h_sc[...] = _rms(x, g_ref[...]).astype(BF16)
        acc_sc[...] = jnp.zeros_like(acc_sc)

    h = h_sc[...]
    gate = _dot(h, wg_ref[...])
    up = _dot(h, wu_ref[...])
    a = gate * _sigmoid(gate) * up
    acc_sc[...] += _dot(a.astype(BF16), wd_ref[...])

    @pl.when(j == pl.num_programs(1) - 1)
    def _():
        y = (o_ref[...] if has_mix else x_ref[...]) + 0.5 * acc_sc[...]
        if final:
            y = _rms(y, gf_ref[...])
        o_ref[...] = y


def _ffn(x, g, wg, wu, wd, mix=None, gf=None, tm=512, tf=512):
    n = x.shape[0]
    tm = min(tm, n)
    grid = (n // tm, D_FF // tf)
    row = lambda i, j: (i, 0)
    in_specs = [pl.BlockSpec((tm, D_MODEL), row), _const_spec((1, D_MODEL)),
                pl.BlockSpec((D_MODEL, tf), lambda i, j: (0, j)), pl.BlockSpec((D_MODEL, tf), lambda i, j: (0, j)),
                pl.BlockSpec((tf, D_MODEL), lambda i, j: (j, 0))]
    args = [x, g, wg, wu, wd]
    if mix is not None:
        o_f, o_n, w_out = mix
        in_specs += [pl.BlockSpec((tm, D_FOX), row), pl.BlockSpec((tm, D_NSA), row),
                     _const_spec((D_FOX + D_NSA, D_MODEL))]
        args += [o_f, o_n, w_out]
    if gf is not None:
        in_specs.append(_const_spec((1, D_MODEL)))
        args.append(gf)
    return pl.pallas_call(
        functools.partial(_ffn_kernel, has_mix=mix is not None, final=gf is not None),
        out_shape=jax.ShapeDtypeStruct((n, D_MODEL), F32),
        grid=grid, in_specs=in_specs, out_specs=pl.BlockSpec((tm, D_MODEL), row),
        scratch_shapes=[pltpu.VMEM((tm, D_MODEL), BF16), pltpu.VMEM((tm, D_MODEL), F32)],
        compiler_params=_cparams(("parallel", "arbitrary")), name="ffn")(*args)


def _proj_kernel(*refs, with_cmp):
    x_ref, g_ref, wa_ref, wb_ref, b_ref, bt_ref, qf_ref, qn_ref, misc_ref, fkvt_ref, nkvt_ref, wint_ref, misct_ref = (
        refs[:13])
    tm = x_ref.shape[0]
    h = _rms(x_ref[...], g_ref[...]).astype(BF16)
    qf_ref[...] = (_dot(h, wa_ref[:, _A_QF:_A_QN]) * SCALE).astype(BF16)
    qn_ref[...] = (_dot(h, wa_ref[:, _A_QN:_A_KCVC]) * SCALE).astype(BF16)
    z = _dot(h, wa_ref[:, _A_MISC:_A_END]) + b_ref[...]
    lane = lax.broadcasted_iota(jnp.int32, z.shape, 1)
    misc_ref[...] = jnp.where(lane < N_FOX, _log_sigmoid(z), _sigmoid(z))
    fkvt_ref[...] = _dot_nt(wb_ref[_B_FKV:_B_NKV, :], h)
    nkvt_ref[...] = _dot_nt(wb_ref[_B_NKV:_B_WIN, :], h)
    wint_ref[...] = _dot_nt(wb_ref[_B_WIN:_B_MISC, :], h)
    zt = _dot_nt(wb_ref[_B_MISC:_B_END, :], h) + bt_ref[...]
    sub = lax.broadcasted_iota(jnp.int32, zt.shape, 0)
    misct_ref[...] = jnp.where(sub < N_FOX, _log_sigmoid(zt), _sigmoid(zt))
    if with_cmp:
        xcmp_ref, tr_sc = refs[13:]
        kcvc = _dot(h, wa_ref[:, _A_KCVC:_A_MISC])
        nch = 2 * D_KV // LANES
        for c in range(nch):
            tr_sc[c] = kcvc[:, c * LANES:(c + 1) * LANES]
        for l in range(STRIDE):
            for c in range(nch):
                rows = tr_sc[c, pl.ds(l, tm // STRIDE, stride=STRIDE), :]
                for i in range(2):
                    xcmp_ref[2 * c + i, :, l * HEAD_DIM:(l + 1) * HEAD_DIM] = (
                        rows[:, i * HEAD_DIM:(i + 1) * HEAD_DIM])


def _proj(x, g, wa, wb, b, bt, with_cmp, tm=256):
    bsz, t, _ = x.shape
    tm = min(tm, t)
    nt = t // tm
    row = lambda bi, i: (bi, i, 0)
    col = lambda bi, i: (bi, 0, i)
    out_shape = [jax.ShapeDtypeStruct((bsz, t, D_FOX), BF16), jax.ShapeDtypeStruct((bsz, t, D_NSA), BF16),
                 jax.ShapeDtypeStruct((bsz, t, MISC_W), F32),
                 jax.ShapeDtypeStruct((bsz, 2 * D_FOX, t), F32), jax.ShapeDtypeStruct((bsz, 4 * D_KV, t), F32),
                 jax.ShapeDtypeStruct((bsz, 2 * D_KV, t), F32), jax.ShapeDtypeStruct((bsz, MISC_W, t), F32)]
    out_specs = [pl.BlockSpec((None, tm, D_FOX), row), pl.BlockSpec((None, tm, D_NSA), row),
                 pl.BlockSpec((None, tm, MISC_W), row),
                 pl.BlockSpec((None, 2 * D_FOX, tm), col), pl.BlockSpec((None, 4 * D_KV, tm), col),
                 pl.BlockSpec((None, 2 * D_KV, tm), col), pl.BlockSpec((None, MISC_W, tm), col)]
    scratch = []
    if with_cmp:
        out_shape.append(jax.ShapeDtypeStruct((2 * N_KV, bsz * t // STRIDE, CMP_ROW), F32))
        out_specs.append(pl.BlockSpec((2 * N_KV, tm // STRIDE, CMP_ROW), lambda bi, i: (0, bi * nt + i, 0)))
        scratch.append(pltpu.VMEM((2 * D_KV // LANES, tm, LANES), F32))
    return pl.pallas_call(
        functools.partial(_proj_kernel, with_cmp=with_cmp), out_shape=tuple(out_shape), grid=(bsz, nt),
        in_specs=[pl.BlockSpec((None, tm, D_MODEL), row), _const_spec((1, D_MODEL)),
                  _const_spec((D_MODEL, _A_END)), _const_spec((_B_END, D_MODEL)),
                  _const_spec((1, MISC_W)), _const_spec((MISC_W, 1))],
        out_specs=tuple(out_specs), scratch_shapes=scratch,
        compiler_params=_cparams(("parallel", "parallel")), name="proj")(x, g, wa, wb, b, bt)


def _cumsum_kernel(lf_ref, tri_ref, c_ref):
    c_ref[...] = _dot3(lf_ref[...], tri_ref[...])


def _cumsum(misct):
    b, _, t = misct.shape
    tri = jnp.triu(jnp.ones((t, t), BF16))
    return pl.pallas_call(
        _cumsum_kernel, out_shape=jax.ShapeDtypeStruct((b, N_FOX, t), F32), grid=(b,),
        in_specs=[pl.BlockSpec((None, N_FOX, t), lambda i: (i, 0, 0)), _const_spec((t, t))],
        out_specs=pl.BlockSpec((None, N_FOX, t), lambda i: (i, 0, 0)),
        compiler_params=_cparams(("parallel",)), name="fox_cumsum")(misct, tri)


def _fox_kernel(q_ref, k_ref, v_ref, c_ref, o_ref, kb_sc, vb_sc, cb_sc, *, tq, nh):
    qi = pl.program_id(2)
    t = k_ref.shape[1]
    tk = tq
    nk = t // tk

    @pl.when(qi == 0)
    def _():
        for j in range(nk):
            cols = slice(j * tk, (j + 1) * tk)
            cb_sc[j] = c_ref[:, cols]
            for h in range(nh):
                rows = slice(h * HEAD_DIM, (h + 1) * HEAD_DIM)
                kb_sc[j, h] = k_ref[rows, cols].astype(BF16)
                vb_sc[j, h] = _aug_values(v_ref[rows, cols])

    q = q_ref[...]
    qs = [q[:, h * HEAD_DIM:(h + 1) * HEAD_DIM] for h in range(nh)]
    causal = (lax.broadcasted_iota(jnp.int32, (tq, tk), 1) <= lax.broadcasted_iota(jnp.int32, (tq, tk), 0))

    def step(kj, carry, diagonal):
        ck = cb_sc[kj]
        out = []
        for h in range(nh):
            s = _dot(qs[h], kb_sc[kj, h]) - ck[h:h + 1, :]
            if diagonal:
                s = jnp.where(causal, s, NEG)
            out.append(_flash_update(s, *carry[h], vb_sc[kj, h]))
        return tuple(out)

    init = tuple((jnp.full((tq, 1), NEG, F32), jnp.zeros((tq, V_AUG), F32)) for _ in range(nh))
    carry = lax.fori_loop(0, qi, functools.partial(step, diagonal=False), init)
    carry = step(qi, carry, True)
    o_ref[...] = jnp.concatenate([_flash_out(acc) for _, acc in carry], axis=1).astype(BF16)


def _fox_prompt(qf, fkvt, crow, tq=512, nh=4):
    b, t, _ = qf.shape
    tq = min(tq, t)
    nk = t // tq
    ng = N_FOX // nh
    w = nh * HEAD_DIM
    crow = crow.reshape(b, ng, nh, t)
    return pl.pallas_call(
        functools.partial(_fox_kernel, tq=tq, nh=nh),
        out_shape=jax.ShapeDtypeStruct((b, t, D_FOX), BF16),
        grid=(b, ng, t // tq),
        in_specs=[pl.BlockSpec((None, tq, w), lambda bi, hg, qi: (bi, qi, hg)),
                  pl.BlockSpec((None, w, t), lambda bi, hg, qi: (bi, hg, 0)),
                  pl.BlockSpec((None, w, t), lambda bi, hg, qi: (bi, ng + hg, 0)),
                  pl.BlockSpec((None, None, nh, t), lambda bi, hg, qi: (bi, hg, 0, 0))],
        out_specs=pl.BlockSpec((None, tq, w), lambda bi, hg, qi: (bi, qi, hg)),
        scratch_shapes=[pltpu.VMEM((nk, nh, HEAD_DIM, tq), BF16), pltpu.VMEM((nk, nh, V_AUG, tq), BF16),
                        pltpu.VMEM((nk, nh, tq), F32)],
        compiler_params=_cparams(("parallel", "parallel", "arbitrary")), name="fox_prompt")(qf, fkvt, fkvt, crow)


def _compress_rows(x, pe_ref, w1_ref, w2_ref):
    r = x.shape[0]
    xa = (x + pe_ref[0:1, :]).astype(BF16)
    xb = (x + pe_ref[1:2, :]).astype(BF16)
    ha = _dot(xa, w1_ref[:CMP_ROW, :])
    hb = _dot(xb, w1_ref[CMP_ROW:, :])
    hid = ha + pltpu.roll(hb, r - 1, 0)
    gl = 0.5 * hid * (1.0 + jnp.tanh(0.7978845608028654 * (hid + 0.044715 * (hid * hid * hid))))
    return _dot(gl.astype(BF16), w2_ref[...])


def _compress_kernel(x_ref, pe_ref, w1_ref, w2_ref, o_ref):
    o_ref[...] = _compress_rows(x_ref[...], pe_ref, w1_ref, w2_ref).astype(BF16)


def _compress_prompt(xcmp, pe2, w1, w2, b, t):
    r = t // STRIDE
    return pl.pallas_call(
        _compress_kernel,
        out_shape=jax.ShapeDtypeStruct((b, 2 * N_KV, r, HEAD_DIM), BF16),
        grid=(b, 2 * N_KV),
        in_specs=[pl.BlockSpec((None, r, CMP_ROW), lambda bi, j: (j, bi, 0)),
                  pl.BlockSpec((None, 2, CMP_ROW), lambda bi, j: (j // N_KV, 0, 0)),
                  pl.BlockSpec((None, L_CMP * HEAD_DIM, LANES), lambda bi, j: (j // N_KV, 0, 0)),
                  pl.BlockSpec((None, LANES, HEAD_DIM), lambda bi, j: (j // N_KV, 0, 0))],
        out_specs=pl.BlockSpec((None, None, r, HEAD_DIM), lambda bi, j: (bi, j, 0, 0)),
        compiler_params=_cparams(("parallel", "parallel")), name="nsa_compress")(xcmp, pe2, w1, w2)


def _slope_col(slopes_ref, g, hrow):
    s0, s1, s2, s3 = (slopes_ref[g, h] for h in range(HPG))
    return jnp.where(hrow == 0, s0, jnp.where(hrow == 1, s1, jnp.where(hrow == 2, s2, s3)))


def _select_blocks(p_slc, cur, nsb, axis):
    jb = lax.broadcasted_iota(jnp.int32, p_slc.shape, axis)
    valid = jb <= cur
    forced = valid & ((jb == 0) | ((cur - jb) < N_LOCAL))
    score = jnp.where(valid, p_slc, -jnp.inf)
    score = jnp.where(forced, jnp.inf, score)
    rank = jnp.zeros(p_slc.shape, jnp.int32)
    for i in range(nsb):
        ci = score[:, i:i + 1] if axis == 1 else score[i:i + 1, :]
        beats = (ci > score) | ((ci == score) & (jb > i))
        rank = rank + jnp.where(beats, 1, 0)
    return jnp.where((rank < N_SEL) & (jb < nsb), 1.0, 0.0)


def _cmp_branch(q4, kc, vc, slope, qpos, tq):
    ncp = kc.shape[0]
    s = _dot_nt(q4, kc)
    cend = lax.broadcasted_iota(jnp.int32, (1, ncp), 1) * STRIDE + (L_CMP - 1)
    s = s - slope * (qpos.astype(F32) - cend.astype(F32))
    pc = _masked_softmax(s, cend <= qpos)
    o_c = _dot(pc.astype(BF16), vc)
    return o_c, pc[0:tq] + pc[tq:2 * tq] + pc[2 * tq:3 * tq] + pc[3 * tq:4 * tq]


def _gate_col(misc, br, g):
    cols = [_lane_pick(misc, GATE_OFF + br * N_NSA + g * HPG + h) for h in range(HPG)]
    return jnp.concatenate(cols, axis=0)


def _slc_matrix(nc, nsb, ncp, width):
    r, c = L_SEL // STRIDE, L_CMP // STRIDE
    offs, counts = np.unique((np.arange(r)[:, None] - np.arange(c)[None, :]).ravel(), return_counts=True)
    m = np.zeros((ncp, width), np.float32)
    for j in range(nsb):
        for off, cnt in zip(offs, counts):
            n = r * j + off
            if 0 <= n < nc:
                m[n, j] += cnt
    return jnp.asarray(m, BF16)


def _alibi_slopes():
    base = 2.0 ** (-8.0 * np.arange(1, N_NSA + 1, dtype=np.float32) / N_NSA)
    return jnp.asarray(base.astype(np.float32).reshape(HPG, N_KV).T)


def _nsa_kernel(slopes_ref, q_ref, misc_ref, cmp_ref, ks_ref, vs_ref, kw_ref, vw_ref, m_ref, e_ref, o_ref,
                ksb, vsb, kwb, vwb, *, tq, tk, nsb):
    g = pl.program_id(1)
    qi = pl.program_id(2)
    t = ks_ref.shape[1]
    nk = t // tk
    wch = LANES
    nwc = t // wch

    @pl.when(qi == 0)
    def _():
        for j in range(nk):
            ksb[j] = ks_ref[:, j * tk:(j + 1) * tk].astype(BF16)
            vsb[j] = _aug_values(vs_ref[:, j * tk:(j + 1) * tk])
        for j in range(nwc):
            kwb[j] = kw_ref[:, j * wch:(j + 1) * wch].astype(BF16)
            vwb[j] = _aug_values(vw_ref[:, j * wch:(j + 1) * wch])

    a = qi * tq
    rows = HPG * tq
    row = lax.broadcasted_iota(jnp.int32, (rows, 1), 0)
    qpos = a + row % tq
    hrow = row // tq
    qpos1 = a + lax.broadcasted_iota(jnp.int32, (tq, 1), 0)
    misc = misc_ref[...]
    slope = _slope_col(slopes_ref, g, hrow)
    q4 = jnp.concatenate([q_ref[:, h * HEAD_DIM:(h + 1) * HEAD_DIM] for h in range(HPG)], axis=0)

    o_c, pg = _cmp_branch(q4, cmp_ref[g], cmp_ref[N_KV + g], slope, qpos, tq)
    hi, mid, lo = _split3(pg)
    mt = m_ref[...]
    p_slct = _dot_nt(mt, hi) + _dot_nt(mt, mid) + _dot_nt(mt, lo)
    cur = (a + lax.broadcasted_iota(jnp.int32, (1, tq), 1)) // L_SEL
    selt = _select_blocks(p_slct, cur, nsb, 0)
    sel = jnp.concatenate([selt, jnp.zeros((LANES - selt.shape[0], tq), F32)], axis=0).T.astype(BF16)

    def step(kj, carry, diagonal):
        kpos = kj * tk + lax.broadcasted_iota(jnp.int32, (1, tk), 1)
        keep = _dot(sel, e_ref[kj]) > 0.5
        if diagonal:
            keep = keep & (kpos <= qpos1)
        s4 = _dot(q4, ksb[kj])
        drel = (kpos - a).astype(F32)
        out = []
        for h in range(HPG):
            s = s4[h * tq:(h + 1) * tq] + slopes_ref[g, h] * drel
            out.append(_flash_update(jnp.where(keep, s, NEG), *carry[h], vsb[kj]))
        return tuple(out)

    init = tuple((jnp.full((tq, 1), NEG, F32), jnp.zeros((tq, V_AUG), F32)) for _ in range(HPG))
    kd = a // tk
    carry = lax.fori_loop(0, kd, functools.partial(step, diagonal=False), init)
    carry = step(kd, carry, True)

    tqc = tq // wch
    nwin = min(WINDOW // wch + tqc, nwc)
    c0 = qi * tqc + tqc - nwin
    rc = (lax.broadcasted_iota(jnp.int32, (tq, wch), 0) - lax.broadcasted_iota(jnp.int32, (tq, wch), 1))
    s_h = [[] for _ in range(HPG)]
    cis = []
    for i in range(nwin):
        exists = c0 + i >= 0
        ci = jnp.maximum(c0 + i, 0)
        cis.append(ci)
        s4 = _dot(q4, kwb[ci])
        rel = (tqc - nwin + i) * wch
        drel = (rel + lax.broadcasted_iota(jnp.int32, (1, wch), 1)).astype(F32)
        keep = exists
        if -(wch - 1) - rel < 0:
            keep = keep & (rc >= rel)
        if (tq - 1) - rel >= WINDOW:
            keep = keep & (rc < rel + WINDOW)
        for h in range(HPG):
            s_h[h].append(jnp.where(keep, s4[h * tq:(h + 1) * tq] + slopes_ref[g, h] * drel, NEG))
    gate = lambda br, h: _lane_pick(misc, GATE_OFF + br * N_NSA + g * HPG + h)
    outs = []
    for h in range(HPG):
        m = jnp.max(functools.reduce(jnp.maximum, s_h[h]), axis=1, keepdims=True)
        acc = _dot_nt(jnp.exp(s_h[h][0] - m).astype(BF16), vwb[cis[0]])
        for i in range(1, nwin):
            acc = acc + _dot_nt(jnp.exp(s_h[h][i] - m).astype(BF16), vwb[cis[i]])
        outs.append(o_c[h * tq:(h + 1) * tq] * gate(0, h) + _flash_out(carry[h][1], gate(1, h))
                    + _flash_out(acc, gate(2, h)))
    o_ref[...] = jnp.concatenate(outs, axis=1).astype(BF16)


def _nsa_prompt(qn, misc, cmp, nkvt, wint, tq=256, tk=512):
    b, t, _ = qn.shape
    tq = min(tq, t)
    tk = min(tk, t)
    assert tk % tq == 0 and tq % LANES == 0
    ncp = t // STRIDE
    nc = (t - L_CMP) // STRIDE + 1
    nsb = -(-t // L_SEL)
    assert nsb <= LANES
    nsbp = -(-nsb // SUBLANES) * SUBLANES
    m = _slc_matrix(nc, nsb, ncp, nsbp).T
    e = (np.arange(t)[None, :] // L_SEL == np.arange(LANES)[:, None]).astype(np.float32)
    e = jnp.asarray(e.reshape(LANES, t // tk, tk).transpose(1, 0, 2), BF16)
    hw = HPG * HEAD_DIM
    grid_spec = pl.GridSpec(
        grid=(b, N_KV, t // tq),
        in_specs=[pl.BlockSpec(memory_space=pltpu.SMEM),
                  pl.BlockSpec((None, tq, hw), lambda bi, g, qi: (bi, qi, g)),
                  pl.BlockSpec((None, tq, MISC_W), lambda bi, g, qi: (bi, qi, 0)),
                  pl.BlockSpec((None, 2 * N_KV, ncp, HEAD_DIM), lambda bi, g, qi: (bi, 0, 0, 0)),
                  pl.BlockSpec((None, HEAD_DIM, t), lambda bi, g, qi: (bi, 2 * N_KV + g, 0)),
                  pl.BlockSpec((None, HEAD_DIM, t), lambda bi, g, qi: (bi, 3 * N_KV + g, 0)),
                  pl.BlockSpec((None, HEAD_DIM, t), lambda bi, g, qi: (bi, g, 0)),
                  pl.BlockSpec((None, HEAD_DIM, t), lambda bi, g, qi: (bi, N_KV + g, 0)),
                  _const_spec((nsbp, ncp)), _const_spec((t // tk, LANES, tk))],
        out_specs=pl.BlockSpec((None, tq, hw), lambda bi, g, qi: (bi, qi, g)),
        scratch_shapes=[pltpu.VMEM((t // tk, HEAD_DIM, tk), BF16), pltpu.VMEM((t // tk, V_AUG, tk), BF16),
                        pltpu.VMEM((t // LANES, HEAD_DIM, LANES), BF16),
                        pltpu.VMEM((t // LANES, V_AUG, LANES), BF16)])
    return pl.pallas_call(
        functools.partial(_nsa_kernel, tq=tq, tk=tk, nsb=nsb),
        out_shape=jax.ShapeDtypeStruct((b, t, D_NSA), BF16), grid_spec=grid_spec,
        compiler_params=_cparams(("parallel", "parallel", "arbitrary")), name="nsa_prompt",
    )(_alibi_slopes(), qn, misc, cmp, nkvt, nkvt, wint, wint, m, e)


def _fox_dec_kernel(*refs, n_new, ppb):
    pt_ref, q_ref = refs[:2]
    kv_refs = refs[2:2 + ppb]
    lf_refs = refs[2 + ppb:2 + 2 * ppb]
    kvn_ref, lfn_ref, gt_ref, le_ref, o_ref, qbd_sc, m_sc, l_sc, r_sc, acc_sc = refs[2 + 2 * ppb:]
    p = pl.program_id(1)
    page = kvn_ref.shape[2]
    rows = n_new * N_FOX

    def tile_heads(x):
        return jnp.concatenate([x] * n_new, axis=0)

    def accumulate(kt, vt, bias, mask):
        s = _dot(qbd_sc[...], kt) + bias
        if mask is not None:
            s = jnp.where(mask, s, NEG)
        m, alpha, l, pr = _online_softmax_step(s, m_sc[...], l_sc[...])
        m_sc[...] = m
        l_sc[...] = l
        acc_sc[...] = alpha * acc_sc[...] + _dot_nt(pr.astype(BF16), vt)

    @pl.when(p == 0)
    def _():
        q = q_ref[...]
        rep = jnp.concatenate([jnp.broadcast_to(q[t:t + 1, :], (N_FOX, D_FOX)) for t in range(n_new)], axis=0)
        rr = lax.broadcasted_iota(jnp.int32, (rows, D_FOX), 0)
        ll = lax.broadcasted_iota(jnp.int32, (rows, D_FOX), 1)
        qbd_sc[...] = jnp.where(ll // HEAD_DIM == rr % N_FOX, rep, 0.0).astype(BF16)
        m_sc[...] = jnp.full(m_sc.shape, NEG, F32)
        l_sc[...] = jnp.zeros(l_sc.shape, F32)
        r_sc[...] = jnp.zeros(r_sc.shape, F32)
        acc_sc[...] = jnp.zeros(acc_sc.shape, F32)
        cn = _dot3(lfn_ref[...], le_ref[...])
        tkey = lax.broadcasted_iota(jnp.int32, (rows, page), 1)
        tqry = lax.broadcasted_iota(jnp.int32, (rows, page), 0) // N_FOX
        accumulate(kvn_ref[0].astype(BF16), kvn_ref[1].astype(BF16), -tile_heads(cn),
                   (tkey <= tqry) & (tkey < n_new))

    later = r_sc[...]
    suffix = []
    for lf_ref in lf_refs:
        lf = lf_ref[...]
        suffix.append(_dot3(lf, gt_ref[...]) + later)
        later = later + jnp.sum(lf, axis=1, keepdims=True)
    r_sc[...] = later
    accumulate(jnp.concatenate([r[0].astype(BF16) for r in kv_refs], axis=1),
               jnp.concatenate([r[1].astype(BF16) for r in kv_refs], axis=1),
               tile_heads(jnp.concatenate(suffix, axis=1)), None)

    @pl.when(p == pl.num_programs(1) - 1)
    def _():
        o = acc_sc[...] / jnp.maximum(l_sc[...], 1e-30)
        rr = lax.broadcasted_iota(jnp.int32, o.shape, 0)
        ll = lax.broadcasted_iota(jnp.int32, o.shape, 1)
        o = jnp.where(ll // HEAD_DIM == rr % N_FOX, o, 0.0)
        orow = lax.broadcasted_iota(jnp.int32, (SQ, D_FOX), 0)
        out = jnp.zeros((SQ, D_FOX), F32)
        for t in range(n_new):
            out = jnp.where(orow == t, jnp.sum(o[t * N_FOX:(t + 1) * N_FOX], axis=0, keepdims=True), out)
        o_ref[...] = out


def _fox_decode(pt_flat, qf_pad, cache_kvt, cache_lft, kvt_new, lft_new, n_pages, n_new):
    db = qf_pad.shape[0]
    page = cache_kvt.shape[3]
    gt = np.tril(np.ones((page, page), np.float32), -1)
    le = np.triu(np.ones((page, page), np.float32))
    ppb = next(c for c in (8, 4, 2, 1) if n_pages % c == 0)
    pidx = lambda i: (lambda b, p, pt: (pt[b * n_pages + n_pages - 1 - (p * ppb + i)], 0, 0, 0))
    pidx3 = lambda i: (lambda b, p, pt: (pt[b * n_pages + n_pages - 1 - (p * ppb + i)], 0, 0))
    rows = n_new * N_FOX
    grid_spec = pltpu.PrefetchScalarGridSpec(
        num_scalar_prefetch=1, grid=(db, n_pages // ppb),
        in_specs=[pl.BlockSpec((None, SQ, D_FOX), lambda b, p, pt: (b, 0, 0))]
        + [pl.BlockSpec((None, 2, D_FOX, page), pidx(i)) for i in range(ppb)]
        + [pl.BlockSpec((None, N_FOX, page), pidx3(i)) for i in range(ppb)]
        + [pl.BlockSpec((None, 2, D_FOX, page), lambda b, p, pt: (b, 0, 0, 0)),
           pl.BlockSpec((None, N_FOX, page), lambda b, p, pt: (b, 0, 0)),
           _const_spec((page, page)), _const_spec((page, page))],
        out_specs=pl.BlockSpec((None, SQ, D_FOX), lambda b, p, pt: (b, 0, 0)),
        scratch_shapes=[pltpu.VMEM((rows, D_FOX), BF16), pltpu.VMEM((rows, 1), F32), pltpu.VMEM((rows, 1), F32),
                        pltpu.VMEM((N_FOX, 1), F32), pltpu.VMEM((rows, D_FOX), F32)])
    return pl.pallas_call(
        functools.partial(_fox_dec_kernel, n_new=n_new, ppb=ppb),
        out_shape=jax.ShapeDtypeStruct((db, SQ, D_FOX), F32), grid_spec=grid_spec,
        compiler_params=_cparams(("parallel", "arbitrary")), name="fox_decode",
    )(pt_flat, qf_pad, *([cache_kvt] * ppb), *([cache_lft] * ppb), kvt_new, lft_new,
      jnp.asarray(gt, BF16), jnp.asarray(le, BF16))


def _nsa_pages_kernel(*refs, ppb):
    pg_refs = refs[1:1 + ppb]
    new_ref, pe_ref, w1_ref, w2_ref, ksvs_ref, cmp_ref, tr_sc, x_sc = refs[1 + ppb:]
    p = pl.program_id(1)
    npg = pl.num_programs(1) - 1
    page = new_ref.shape[1]
    rpp = page // STRIDE
    nch = D_KV // LANES

    @pl.when(p < npg)
    def _():
        ksvs_ref[...] = jnp.concatenate(
            [jnp.concatenate([r[2], r[3]], axis=0) for r in pg_refs], axis=1).astype(BF16)
        for k, pg_ref in enumerate(pg_refs):
            for s in range(2):
                for c in range(nch):
                    tr_sc[(k * 2 + s) * nch + c] = pg_ref[s, c * LANES:(c + 1) * LANES, :].T
            r0 = pl.multiple_of((p * ppb + k) * rpp, rpp)
            for l in range(STRIDE):
                for c in range(2 * nch):
                    rows = tr_sc[k * 2 * nch + c, pl.ds(l, rpp, stride=STRIDE), :]
                    for i in range(2):
                        x_sc[2 * c + i, pl.ds(r0, rpp), l * HEAD_DIM:(l + 1) * HEAD_DIM] = (
                            rows[:, i * HEAD_DIM:(i + 1) * HEAD_DIM])

    @pl.when(p == npg)
    def _():
        pad = [jnp.zeros((2 * D_KV, (ppb - 1) * page), F32)] if ppb > 1 else []
        ksvs_ref[...] = jnp.concatenate([new_ref[...]] + pad, axis=1).astype(BF16)
        for j in range(2 * N_KV):
            s = j // N_KV
            cmp_ref[j] = _compress_rows(x_sc[j], pe_ref.at[s], w1_ref.at[s], w2_ref.at[s]).astype(BF16)


def _nsa_pages(pt_flat, cache_nsat, ksvst_new, pe2, w1, w2, n_pages):
    db = ksvst_new.shape[0]
    page = cache_nsat.shape[3]
    past = n_pages * page
    rows = past // STRIDE
    ppb = next(c for c in (8, 4, 2, 1) if n_pages % c == 0)
    nsteps = n_pages // ppb
    pidx = lambda i: (lambda b, p, pt: (pt[b * n_pages + jnp.minimum(p * ppb + i, n_pages - 1)], 0, 0, 0))
    grid_spec = pltpu.PrefetchScalarGridSpec(
        num_scalar_prefetch=1, grid=(db, nsteps + 1),
        in_specs=[pl.BlockSpec((None, 4, D_KV, page), pidx(i)) for i in range(ppb)]
        + [pl.BlockSpec((None, 2 * D_KV, page), lambda b, p, pt: (b, 0, 0)),
           _const_spec((2, 2, CMP_ROW)), _const_spec((2, L_CMP * HEAD_DIM, LANES)),
           _const_spec((2, LANES, HEAD_DIM))],
        out_specs=(pl.BlockSpec((None, 2 * D_KV, ppb * page), lambda b, p, pt: (b, 0, p)),
                   pl.BlockSpec((None, 2 * N_KV, rows, HEAD_DIM), lambda b, p, pt: (b, 0, 0, 0))),
        scratch_shapes=[pltpu.VMEM((ppb * 2 * D_KV // LANES, page, LANES), F32),
                        pltpu.VMEM((2 * N_KV, rows, CMP_ROW), F32)])
    return pl.pallas_call(
        functools.partial(_nsa_pages_kernel, ppb=ppb),
        out_shape=(jax.ShapeDtypeStruct((db, 2 * D_KV, past + ppb * page), BF16),
                   jax.ShapeDtypeStruct((db, 2 * N_KV, rows, HEAD_DIM), BF16)),
        grid_spec=grid_spec, compiler_params=_cparams(("parallel", "arbitrary")), name="nsa_pages",
    )(pt_flat, *([cache_nsat] * ppb), ksvst_new, pe2, w1, w2)


def _nsa_dec_kernel(slopes_ref, q_ref, misc_ref, cmp_ref, ksvs_ref, win_ref, winn_ref, m_ref, e_ref, o_ref,
                    selx_sc, *, past, nsb):
    tq = SQ
    rows = HPG * tq
    row = lax.broadcasted_iota(jnp.int32, (rows, 1), 0)
    qpos = past + row % tq
    qposf = qpos.astype(F32)
    hrow = row // tq
    misc = misc_ref[...]
    q = q_ref[...]
    cur = (past + lax.broadcasted_iota(jnp.int32, (tq, 1), 0)) // L_SEL
    nkeys = ksvs_ref.shape[1]

    o_cs, sels, q4s, slopes = [], [], [], []
    for g in range(N_KV):
        slope = _slope_col(slopes_ref, g, hrow)
        q4 = jnp.concatenate(
            [q[:, (g * HPG + h) * HEAD_DIM:(g * HPG + h + 1) * HEAD_DIM] for h in range(HPG)], axis=0).astype(BF16)
        o_c, pg = _cmp_branch(q4, cmp_ref[g], cmp_ref[N_KV + g], slope, qpos, tq)
        sels.append(_select_blocks(_dot3(pg, m_ref[...]), cur, nsb, 1))
        o_cs.append(o_c)
        q4s.append(q4)
        slopes.append(slope)
    selx_sc[...] = _dot(jnp.concatenate(sels, axis=0).astype(BF16), e_ref[...])

    kpos = lax.broadcasted_iota(jnp.int32, (1, nkeys), 1)
    nwin = win_ref.shape[2]
    wpos = (past - nwin) + lax.broadcasted_iota(jnp.int32, (1, nwin + winn_ref.shape[2]), 1)
    outs = []
    for g in range(N_KV):
        q4, slope = q4s[g], slopes[g]
        gr = slice(g * HEAD_DIM, (g + 1) * HEAD_DIM)
        s = _dot(q4, ksvs_ref[gr, :]) - slope * (qposf - kpos.astype(F32))
        selx = selx_sc[g * tq:(g + 1) * tq, :]
        mask = (jnp.concatenate([selx] * HPG, axis=0) > 0.5) & (kpos <= qpos)
        o_s = _dot_nt(_masked_softmax(s, mask).astype(BF16), ksvs_ref[D_KV + g * HEAD_DIM:D_KV + (g + 1) * HEAD_DIM, :])
        kw = jnp.concatenate([win_ref[0, gr, :], winn_ref[0, gr, :]], axis=1).astype(BF16)
        vw = jnp.concatenate([win_ref[1, gr, :], winn_ref[1, gr, :]], axis=1).astype(BF16)
        dist = qpos - wpos
        s = _dot(q4, kw) - slope * dist.astype(F32)
        o_w = _dot_nt(_masked_softmax(s, (dist >= 0) & (dist < WINDOW)).astype(BF16), vw)
        o = _gate_col(misc, 0, g) * o_cs[g] + _gate_col(misc, 1, g) * o_s + _gate_col(misc, 2, g) * o_w
        outs += [o[h * tq:(h + 1) * tq] for h in range(HPG)]
    o_ref[...] = jnp.concatenate(outs, axis=1)


def _nsa_decode(qn_pad, misc_pad, cmp, ksvst, wint_state, wint_new, past, n_new):
    db = qn_pad.shape[0]
    nkeys = ksvst.shape[2]
    ncp = cmp.shape[2]
    tk_all = past + n_new
    nc = (tk_all - L_CMP) // STRIDE + 1
    nsb = -(-tk_all // L_SEL)
    width = -(-nsb // LANES) * LANES
    m = _slc_matrix(nc, nsb, ncp, width)
    e = jnp.asarray(np.arange(nkeys)[None, :] // L_SEL == np.arange(width)[:, None], BF16)
    nwin = wint_state.shape[3]
    nwn = wint_new.shape[3]
    per_b3 = lambda b: (b, 0, 0)
    per_b4 = lambda b: (b, 0, 0, 0)
    grid_spec = pl.GridSpec(
        grid=(db,),
        in_specs=[pl.BlockSpec(memory_space=pltpu.SMEM),
                  pl.BlockSpec((None, SQ, D_NSA), per_b3), pl.BlockSpec((None, SQ, MISC_W), per_b3),
                  pl.BlockSpec((None, 2 * N_KV, ncp, HEAD_DIM), per_b4),
                  pl.BlockSpec((None, 2 * D_KV, nkeys), per_b3),
                  pl.BlockSpec((None, 2, D_KV, nwin), per_b4), pl.BlockSpec((None, 2, D_KV, nwn), per_b4),
                  _const_spec((ncp, width)), _const_spec((width, nkeys))],
        out_specs=pl.BlockSpec((None, SQ, D_NSA), per_b3),
        scratch_shapes=[pltpu.VMEM((N_KV * SQ, nkeys), F32)])
    return pl.pallas_call(
        functools.partial(_nsa_dec_kernel, past=past, nsb=nsb),
        out_shape=jax.ShapeDtypeStruct((db, SQ, D_NSA), F32), grid_spec=grid_spec,
        compiler_params=_cparams(("parallel",)), name="nsa_decode",
    )(_alibi_slopes(), qn_pad, misc_pad, cmp, ksvst, wint_state, wint_new, m, e)


def _prep_w_in(w_in, b_forget, b_gate):
    sizes = (D_FOX, D_FOX, D_FOX, N_FOX, D_NSA) + (D_KV,) * 6 + (3 * N_NSA,)
    offs = np.cumsum((0,) + sizes)
    part = lambda i, j=None: w_in[:, offs[i]:offs[(i if j is None else j) + 1]]
    pad = MISC_W - N_FOX - 3 * N_NSA
    misc = jnp.concatenate([part(3), part(11), jnp.zeros((w_in.shape[0], pad), w_in.dtype)], axis=1)
    wa = jnp.concatenate([part(0), part(4), part(5, 6), misc], axis=1).astype(BF16)
    wb = jnp.concatenate([part(1, 2), part(5, 10), misc], axis=1).T.astype(BF16)
    b = jnp.concatenate([b_forget, b_gate, jnp.zeros((pad,), F32)]).astype(F32)
    return wa, wb, b.reshape(1, MISC_W), b.reshape(MISC_W, 1)


def _pos_minor(x):
    nd = x.ndim
    return jnp.transpose(x, tuple(range(nd - 4)) + (nd - 3, nd - 2, nd - 1, nd - 4))


def _pos_major(x):
    nd = x.ndim
    return jnp.transpose(x, tuple(range(nd - 4)) + (nd - 1, nd - 4, nd - 3, nd - 2))


def kernel(x_prompt, x_sample, cache_fox_kv, cache_fox_logf, cache_nsa_kv, state_win_kv, page_table, norm_ffn1,
           ffn1_gate, ffn1_up, ffn1_down, norm_mix, w_in, b_forget, b_gate, phi_pe, phi_w1, phi_w2, w_out,
           norm_ffn2, ffn2_gate, ffn2_up, ffn2_down, norm_final):
    depth = norm_ffn1.shape[0]
    bsz, t, _ = x_prompt.shape
    db, s_new, _ = x_sample.shape
    n_pool, page = cache_fox_kv.shape[1], cache_fox_kv.shape[2]
    n_pages = page_table.shape[1]
    past = n_pages * page
    wb_len = state_win_kv.shape[2]
    wp = min(WINDOW, t)
    assert s_new <= SQ and wb_len == min(WINDOW, past)

    xp = x_prompt
    xs = x_sample.reshape(1, db * s_new, D_MODEL)
    fox_kvt_pool = _pos_minor(cache_fox_kv).reshape(depth * n_pool, 2, D_FOX, page)
    fox_lft_pool = jnp.swapaxes(cache_fox_logf, 2, 3).reshape(depth * n_pool, N_FOX, page)
    nsa_kvt_pool = _pos_minor(cache_nsa_kv).reshape(depth * n_pool, 4, D_KV, page)
    win_t = _pos_minor(state_win_kv).reshape(depth, db, 2, D_KV, wb_len)
    gfin = norm_final.reshape(1, D_MODEL)
    outs = [[] for _ in range(8)]

    def pad_rows(a):
        a = a.reshape(db, s_new, -1).astype(F32)
        return jnp.pad(a, ((0, 0), (0, SQ - s_new), (0, 0)))

    def new_cols(a_t, lead):
        a_t = jnp.transpose(a_t.reshape(a_t.shape[1], db, s_new), (1, 0, 2))
        return jnp.pad(a_t, ((0, 0), (0, 0), (0, page - s_new))).reshape((db,) + lead + (page,))

    for l in range(depth):
        last = l == depth - 1
        g1 = norm_ffn1[l].reshape(1, D_MODEL)
        gm = norm_mix[l].reshape(1, D_MODEL)
        g2 = norm_ffn2[l].reshape(1, D_MODEL)
        ffn1 = (ffn1_gate[l].astype(BF16), ffn1_up[l].astype(BF16), ffn1_down[l].astype(BF16))
        ffn2 = (ffn2_gate[l].astype(BF16), ffn2_up[l].astype(BF16), ffn2_down[l].astype(BF16))
        wa, wbt, b_r, b_c = _prep_w_in(w_in[l], b_forget[l], b_gate[l])
        wo = w_out[l].astype(BF16)
        pe2 = phi_pe[l].reshape(2, 2, CMP_ROW)
        w1 = phi_w1[l].astype(BF16)
        w2 = phi_w2[l].astype(BF16)
        pt_flat = (page_table + l * n_pool).reshape(-1).astype(jnp.int32)

        xp = _ffn(xp.reshape(bsz * t, D_MODEL), g1, *ffn1).reshape(bsz, t, D_MODEL)
        xs = _ffn(xs.reshape(db * s_new, D_MODEL), g1, *ffn1).reshape(1, db * s_new, D_MODEL)

        qf, qn, misc, fkvt, nkvt, wint, misct, xcmp = _proj(xp, gm, wa, wbt, b_r, b_c, True)
        o_f = _fox_prompt(qf, fkvt, _cumsum(misct))
        cmp = _compress_prompt(xcmp, pe2, w1, w2, bsz, t)
        o_n = _nsa_prompt(qn, misc, cmp, nkvt, wint)
        xp = _ffn(xp.reshape(bsz * t, D_MODEL), g2, *ffn2,
                  mix=(o_f.reshape(bsz * t, D_FOX), o_n.reshape(bsz * t, D_NSA), wo),
                  gf=gfin if last else None).reshape(bsz, t, D_MODEL)
        outs[0].append(_pos_major(fkvt.reshape(bsz, 2, N_FOX, HEAD_DIM, t)))
        outs[1].append(jnp.swapaxes(misct[:, :N_FOX, :], 1, 2))
        outs[2].append(_pos_major(nkvt.reshape(bsz, 4, N_KV, HEAD_DIM, t)))
        outs[3].append(_pos_major(wint[:, :, t - wp:].reshape(bsz, 2, N_KV, HEAD_DIM, wp)))

        qf, qn, misc, fkvt, nkvt, wint, misct = _proj(xs, gm, wa, wbt, b_r, b_c, False)
        o_f = _fox_decode(pt_flat, pad_rows(qf), fox_kvt_pool, fox_lft_pool, new_cols(fkvt, (2, D_FOX)),
                          new_cols(misct[:, :N_FOX, :], (N_FOX,)), n_pages, s_new)
        ksvst, cmp = _nsa_pages(pt_flat, nsa_kvt_pool, new_cols(nkvt[:, 2 * D_KV:, :], (2 * D_KV,)),
                                pe2, w1, w2, n_pages)
        o_n = _nsa_decode(pad_rows(qn), pad_rows(misc), cmp, ksvst, win_t[l], new_cols(wint, (2, D_KV)),
                          past, s_new)
        o_f = o_f[:, :s_new].reshape(db * s_new, D_FOX).astype(BF16)
        o_n = o_n[:, :s_new].reshape(db * s_new, D_NSA).astype(BF16)
        xs = _ffn(xs.reshape(db * s_new, D_MODEL), g2, *ffn2, mix=(o_f, o_n, wo),
                  gf=gfin if last else None).reshape(1, db * s_new, D_MODEL)
        tok_major = lambda a_t, lead: jnp.transpose(a_t[0], (1, 0)).reshape((db, s_new) + lead)
        win_new = tok_major(wint, (2, N_KV, HEAD_DIM))
        outs[4].append(tok_major(fkvt, (2, N_FOX, HEAD_DIM)))
        outs[5].append(tok_major(misct[:, :N_FOX, :], (N_FOX,)))
        outs[6].append(tok_major(nkvt, (4, N_KV, HEAD_DIM)))
        outs[7].append(jnp.concatenate([state_win_kv[l], win_new], axis=1)[:, -wb_len:])

    y_prompt = xp
    y_sample = xs.reshape(db, s_new, D_MODEL)
    return (y_prompt, y_sample) + tuple(jnp.stack(o) for o in outs)
```

```python
import functools

import numpy as np
import jax
import jax.numpy as jnp
from jax import lax
from jax.experimental import pallas as pl
from jax.experimental.pallas import tpu as pltpu

F32 = jnp.float32
BF16 = jnp.bfloat16

D_MODEL = 2048
HEAD_DIM = 64
N_FOX = 16
N_NSA = 16
N_KV = 4
HPG = N_NSA // N_KV
D_FOX = N_FOX * HEAD_DIM
D_NSA = N_NSA * HEAD_DIM
D_KV = N_KV * HEAD_DIM
D_FF = 5632
L_CMP = 32
STRIDE = 16
L_SEL = 64
N_SEL = 16
N_LOCAL = 2
WINDOW = 512
RMS_EPS = 1e-6
NEG = -1e30
SCALE = HEAD_DIM ** -0.5

LANES = 128
SUBLANES = 8
MISC_W = LANES
GATE_OFF = N_FOX
SQ = SUBLANES
CMP_ROW = STRIDE * HEAD_DIM
V_AUG = HEAD_DIM + 2 * SUBLANES

_A_QF = 0
_A_QN = _A_QF + D_FOX
_A_KCVC = _A_QN + D_NSA
_A_MISC = _A_KCVC + 2 * D_KV
_A_END = _A_MISC + MISC_W
_B_FKV = 0
_B_NKV = _B_FKV + 2 * D_FOX
_B_WIN = _B_NKV + 4 * D_KV
_B_MISC = _B_WIN + 2 * D_KV
_B_END = _B_MISC + MISC_W

_VMEM_LIMIT = 56 * 1024 * 1024


def _cparams(sem):
    return pltpu.CompilerParams(dimension_semantics=sem, vmem_limit_bytes=_VMEM_LIMIT)


def _const_spec(shape):
    nd = len(shape)
    return pl.BlockSpec(shape, lambda *_: (0,) * nd, pipeline_mode=pl.Buffered(1))


def _dot(a, b):
    return jnp.dot(a, b, preferred_element_type=F32)


def _dot_nt(a, b):
    return lax.dot_general(a, b, (((1,), (1,)), ((), ())), preferred_element_type=F32)


def _split3(x):
    hi = x.astype(BF16)
    r1 = x - hi.astype(F32)
    mid = r1.astype(BF16)
    lo = (r1 - mid.astype(F32)).astype(BF16)
    return hi, mid, lo


def _dot3(x, m):
    hi, mid, lo = _split3(x)
    return _dot(hi, m) + _dot(mid, m) + _dot(lo, m)


def _rms(x, g):
    return x * lax.rsqrt(jnp.mean(x * x, axis=-1, keepdims=True) + RMS_EPS) * g


def _masked_softmax(s, mask):
    s = jnp.where(mask, s, NEG)
    e = jnp.where(mask, jnp.exp(s - jnp.max(s, axis=-1, keepdims=True)), 0.0)
    return e / jnp.maximum(jnp.sum(e, axis=-1, keepdims=True), 1e-30)


def _online_softmax_step(s, m, l):
    m_new = jnp.maximum(m, jnp.max(s, axis=1, keepdims=True))
    p = jnp.exp(s - m_new)
    alpha = jnp.exp(m - m_new)
    return m_new, alpha, alpha * l + jnp.sum(p, axis=1, keepdims=True), p


def _aug_values(vt):
    extra = lax.broadcasted_iota(jnp.int32, (V_AUG - HEAD_DIM, vt.shape[1]), 0) == 0
    return jnp.concatenate([vt, jnp.where(extra, 1.0, 0.0)], axis=0).astype(BF16)


def _flash_update(s, m, acc, vt_aug):
    m_new = jnp.maximum(m, jnp.max(s, axis=1, keepdims=True))
    p = jnp.exp(s - m_new)
    return m_new, jnp.exp(m - m_new) * acc + _dot_nt(p.astype(BF16), vt_aug)


def _flash_out(acc, gate=1.0):
    return acc[:, :HEAD_DIM] * (gate / jnp.maximum(acc[:, HEAD_DIM:HEAD_DIM + 1], 1e-30))


def _lane_pick(x, idx):
    lane = lax.broadcasted_iota(jnp.int32, x.shape, 1)
    return jnp.sum(jnp.where(lane == idx, x, 0.0), axis=1, keepdims=True)


def _log_sigmoid(z):
    return jnp.minimum(z, 0.0) - jnp.log1p(jnp.exp(-jnp.abs(z)))


def _sigmoid(z):
    return 1.0 / (1.0 + jnp.exp(-z))


def _ffn_kernel(*refs, has_mix, final):
    it = iter(refs)
    x_ref, g_ref, wg_ref, wu_ref, wd_ref = (next(it) for _ in range(5))
    if has_mix:
        of_ref, on_ref, wo_ref = (next(it) for _ in range(3))
    if final:
        gf_ref = next(it)
    o_ref, h_sc, acc_sc = (next(it) for _ in range(3))
    j = pl.program_id(1)

    @pl.when(j == 0)
    def _():
        x = x_ref[...]
        if has_mix:
            x = x + _dot(of_ref[...], wo_ref[:D_FOX, :]) + _dot(on_ref[...], wo_ref[D_FOX:, :])
            o_ref[...] = x
        h_sc[...] = _rms(x, g_ref[...]).astype(BF16)
        acc_sc[...] = jnp.zeros_like(acc_sc)

    h = h_sc[...]
    gate = _dot(h, wg_ref[...])
    up = _dot(h, wu_ref[...])
    a = gate * _sigmoid(gate) * up
    acc_sc[...] += _dot(a.astype(BF16), wd_ref[...])

    @pl.when(j == pl.num_programs(1) - 1)
    def _():
        y = (o_ref[...] if has_mix else x_ref[...]) + 0.5 * acc_sc[...]
        if final:
            y = _rms(y, gf_ref[...])
        o_ref[...] = y


def _ffn(x, g, wg, wu, wd, mix=None, gf=None, tm=512, tf=512):
    n = x.shape[0]
    tm = min(tm, n)
    grid = (n // tm, D_FF // tf)
    row = lambda i, j: (i, 0)
    in_specs = [pl.BlockSpec((tm, D_MODEL), row), _const_spec((1, D_MODEL)),
                pl.BlockSpec((D_MODEL, tf), lambda i, j: (0, j)), pl.BlockSpec((D_MODEL, tf), lambda i, j: (0, j)),
                pl.BlockSpec((tf, D_MODEL), lambda i, j: (j, 0))]
    args = [x, g, wg, wu, wd]
    if mix is not None:
        o_f, o_n, w_out = mix
        in_specs += [pl.BlockSpec((tm, D_FOX), row), pl.BlockSpec((tm, D_NSA), row),
                     _const_spec((D_FOX + D_NSA, D_MODEL))]
        args += [o_f, o_n, w_out]
    if gf is not None:
        in_specs.append(_const_spec((1, D_MODEL)))
        args.append(gf)
    return pl.pallas_call(
        functools.partial(_ffn_kernel, has_mix=mix is not None, final=gf is not None),
        out_shape=jax.ShapeDtypeStruct((n, D_MODEL), F32),
        grid=grid, in_specs=in_specs, out_specs=pl.BlockSpec((tm, D_MODEL), row),
        scratch_shapes=[pltpu.VMEM((tm, D_MODEL), BF16), pltpu.VMEM((tm, D_MODEL), F32)],
        compiler_params=_cparams(("parallel", "arbitrary")), name="ffn")(*args)


def _proj_kernel(*refs, with_cmp):
    x_ref, g_ref, wa_ref, wb_ref, b_ref, bt_ref, qf_ref, qn_ref, misc_ref, fkvt_ref, nkvt_ref, wint_ref, misct_ref = (
        refs[:13])
    tm = x_ref.shape[0]
    h = _rms(x_ref[...], g_ref[...]).astype(BF16)
    qf_ref[...] = (_dot(h, wa_ref[:, _A_QF:_A_QN]) * SCALE).astype(BF16)
    qn_ref[...] = (_dot(h, wa_ref[:, _A_QN:_A_KCVC]) * SCALE).astype(BF16)
    z = _dot(h, wa_ref[:, _A_MISC:_A_END]) + b_ref[...]
    lane = lax.broadcasted_iota(jnp.int32, z.shape, 1)
    misc_ref[...] = jnp.where(lane < N_FOX, _log_sigmoid(z), _sigmoid(z))
    fkvt_ref[...] = _dot_nt(wb_ref[_B_FKV:_B_NKV, :], h)
    nkvt_ref[...] = _dot_nt(wb_ref[_B_NKV:_B_WIN, :], h)
    wint_ref[...] = _dot_nt(wb_ref[_B_WIN:_B_MISC, :], h)
    zt = _dot_nt(wb_ref[_B_MISC:_B_END, :], h) + bt_ref[...]
    sub = lax.broadcasted_iota(jnp.int32, zt.shape, 0)
    misct_ref[...] = jnp.where(sub < N_FOX, _log_sigmoid(zt), _sigmoid(zt))
    if with_cmp:
        xcmp_ref, tr_sc = refs[13:]
        kcvc = _dot(h, wa_ref[:, _A_KCVC:_A_MISC])
        nch = 2 * D_KV // LANES
        for c in range(nch):
            tr_sc[c] = kcvc[:, c * LANES:(c + 1) * LANES]
        for l in range(STRIDE):
            for c in range(nch):
                rows = tr_sc[c, pl.ds(l, tm // STRIDE, stride=STRIDE), :]
                for i in range(2):
                    xcmp_ref[2 * c + i, :, l * HEAD_DIM:(l + 1) * HEAD_DIM] = (
                        rows[:, i * HEAD_DIM:(i + 1) * HEAD_DIM])


def _proj(x, g, wa, wb, b, bt, with_cmp, tm=256):
    bsz, t, _ = x.shape
    tm = min(tm, t)
    nt = t // tm
    row = lambda bi, i: (bi, i, 0)
    col = lambda bi, i: (bi, 0, i)
    out_shape = [jax.ShapeDtypeStruct((bsz, t, D_FOX), BF16), jax.ShapeDtypeStruct((bsz, t, D_NSA), BF16),
                 jax.ShapeDtypeStruct((bsz, t, MISC_W), F32),
                 jax.ShapeDtypeStruct((bsz, 2 * D_FOX, t), F32), jax.ShapeDtypeStruct((bsz, 4 * D_KV, t), F32),
                 jax.ShapeDtypeStruct((bsz, 2 * D_KV, t), F32), jax.ShapeDtypeStruct((bsz, MISC_W, t), F32)]
    out_specs = [pl.BlockSpec((None, tm, D_FOX), row), pl.BlockSpec((None, tm, D_NSA), row),
                 pl.BlockSpec((None, tm, MISC_W), row),
                 pl.BlockSpec((None, 2 * D_FOX, tm), col), pl.BlockSpec((None, 4 * D_KV, tm), col),
                 pl.BlockSpec((None, 2 * D_KV, tm), col), pl.BlockSpec((None, MISC_W, tm), col)]
    scratch = []
    if with_cmp:
        out_shape.append(jax.ShapeDtypeStruct((2 * N_KV, bsz * t // STRIDE, CMP_ROW), F32))
        out_specs.append(pl.BlockSpec((2 * N_KV, tm // STRIDE, CMP_ROW), lambda bi, i: (0, bi * nt + i, 0)))
        scratch.append(pltpu.VMEM((2 * D_KV // LANES, tm, LANES), F32))
    return pl.pallas_call(
        functools.partial(_proj_kernel, with_cmp=with_cmp), out_shape=tuple(out_shape), grid=(bsz, nt),
        in_specs=[pl.BlockSpec((None, tm, D_MODEL), row), _const_spec((1, D_MODEL)),
                  _const_spec((D_MODEL, _A_END)), _const_spec((_B_END, D_MODEL)),
                  _const_spec((1, MISC_W)), _const_spec((MISC_W, 1))],
        out_specs=tuple(out_specs), scratch_shapes=scratch,
        compiler_params=_cparams(("parallel", "parallel")), name="proj")(x, g, wa, wb, b, bt)


def _cumsum_kernel(lf_ref, tri_ref, c_ref):
    c_ref[...] = _dot3(lf_ref[...], tri_ref[...])


def _cumsum(misct):
    b, _, t = misct.shape
    tri = jnp.triu(jnp.ones((t, t), BF16))
    return pl.pallas_call(
        _cumsum_kernel, out_shape=jax.ShapeDtypeStruct((b, N_FOX, t), F32), grid=(b,),
        in_specs=[pl.BlockSpec((None, N_FOX, t), lambda i: (i, 0, 0)), _const_spec((t, t))],
        out_specs=pl.BlockSpec((None, N_FOX, t), lambda i: (i, 0, 0)),
        compiler_params=_cparams(("parallel",)), name="fox_cumsum")(misct, tri)


def _fox_kernel(q_ref, k_ref, v_ref, c_ref, o_ref, kb_sc, vb_sc, cb_sc, *, tq, nh):
    qi = pl.program_id(2)
    t = k_ref.shape[1]
    tk = tq
    nk = t // tk

    @pl.when(qi == 0)
    def _():
        for j in range(nk):
            cols = slice(j * tk, (j + 1) * tk)
            cb_sc[j] = c_ref[:, cols]
            for h in range(nh):
                rows = slice(h * HEAD_DIM, (h + 1) * HEAD_DIM)
                kb_sc[j, h] = k_ref[rows, cols].astype(BF16)
                vb_sc[j, h] = _aug_values(v_ref[rows, cols])

    q = q_ref[...]
    qs = [q[:, h * HEAD_DIM:(h + 1) * HEAD_DIM] for h in range(nh)]
    causal = (lax.broadcasted_iota(jnp.int32, (tq, tk), 1) <= lax.broadcasted_iota(jnp.int32, (tq, tk), 0))

    def step(kj, carry, diagonal):
        ck = cb_sc[kj]
        out = []
        for h in range(nh):
            s = _dot(qs[h], kb_sc[kj, h]) - ck[h:h + 1, :]
            if diagonal:
                s = jnp.where(causal, s, NEG)
            out.append(_flash_update(s, *carry[h], vb_sc[kj, h]))
        return tuple(out)

    init = tuple((jnp.full((tq, 1), NEG, F32), jnp.zeros((tq, V_AUG), F32)) for _ in range(nh))
    carry = lax.fori_loop(0, qi, functools.partial(step, diagonal=False), init)
    carry = step(qi, carry, True)
    o_ref[...] = jnp.concatenate([_flash_out(acc) for _, acc in carry], axis=1).astype(BF16)


def _fox_prompt(qf, fkvt, crow, tq=512, nh=4):
    b, t, _ = qf.shape
    tq = min(tq, t)
    nk = t // tq
    ng = N_FOX // nh
    w = nh * HEAD_DIM
    crow = crow.reshape(b, ng, nh, t)
    return pl.pallas_call(
        functools.partial(_fox_kernel, tq=tq, nh=nh),
        out_shape=jax.ShapeDtypeStruct((b, t, D_FOX), BF16),
        grid=(b, ng, t // tq),
        in_specs=[pl.BlockSpec((None, tq, w), lambda bi, hg, qi: (bi, qi, hg)),
                  pl.BlockSpec((None, w, t), lambda bi, hg, qi: (bi, hg, 0)),
                  pl.BlockSpec((None, w, t), lambda bi, hg, qi: (bi, ng + hg, 0)),
                  pl.BlockSpec((None, None, nh, t), lambda bi, hg, qi: (bi, hg, 0, 0))],
        out_specs=pl.BlockSpec((None, tq, w), lambda bi, hg, qi: (bi, qi, hg)),
        scratch_shapes=[pltpu.VMEM((nk, nh, HEAD_DIM, tq), BF16), pltpu.VMEM((nk, nh, V_AUG, tq), BF16),
                        pltpu.VMEM((nk, nh, tq), F32)],
        compiler_params=_cparams(("parallel", "parallel", "arbitrary")), name="fox_prompt")(qf, fkvt, fkvt, crow)


def _compress_rows(x, pe_ref, w1_ref, w2_ref):
    r = x.shape[0]
    xa = (x + pe_ref[0:1, :]).astype(BF16)
    xb = (x + pe_ref[1:2, :]).astype(BF16)
    ha = _dot(xa, w1_ref[:CMP_ROW, :])
    hb = _dot(xb, w1_ref[CMP_ROW:, :])
    hid = ha + pltpu.roll(hb, r - 1, 0)
    gl = 0.5 * hid * (1.0 + jnp.tanh(0.7978845608028654 * (hid + 0.044715 * (hid * hid * hid))))
    return _dot(gl.astype(BF16), w2_ref[...])


def _compress_kernel(x_ref, pe_ref, w1_ref, w2_ref, o_ref):
    o_ref[...] = _compress_rows(x_ref[...], pe_ref, w1_ref, w2_ref).astype(BF16)


def _compress_prompt(xcmp, pe2, w1, w2, b, t):
    r = t // STRIDE
    return pl.pallas_call(
        _compress_kernel,
        out_shape=jax.ShapeDtypeStruct((b, 2 * N_KV, r, HEAD_DIM), BF16),
        grid=(b, 2 * N_KV),
        in_specs=[pl.BlockSpec((None, r, CMP_ROW), lambda bi, j: (j, bi, 0)),
                  pl.BlockSpec((None, 2, CMP_ROW), lambda bi, j: (j // N_KV, 0, 0)),
                  pl.BlockSpec((None, L_CMP * HEAD_DIM, LANES), lambda bi, j: (j // N_KV, 0, 0)),
                  pl.BlockSpec((None, LANES, HEAD_DIM), lambda bi, j: (j // N_KV, 0, 0))],
        out_specs=pl.BlockSpec((None, None, r, HEAD_DIM), lambda bi, j: (bi, j, 0, 0)),
        compiler_params=_cparams(("parallel", "parallel")), name="nsa_compress")(xcmp, pe2, w1, w2)


def _slope_col(slopes_ref, g, hrow):
    s0, s1, s2, s3 = (slopes_ref[g, h] for h in range(HPG))
    return jnp.where(hrow == 0, s0, jnp.where(hrow == 1, s1, jnp.where(hrow == 2, s2, s3)))


def _select_blocks(p_slc, cur, nsb, axis):
    jb = lax.broadcasted_iota(jnp.int32, p_slc.shape, axis)
    valid = jb <= cur
    forced = valid & ((jb == 0) | ((cur - jb) < N_LOCAL))
    score = jnp.where(valid, p_slc, -jnp.inf)
    score = jnp.where(forced, jnp.inf, score)
    rank = jnp.zeros(p_slc.shape, jnp.int32)
    for i in range(nsb):
        ci = score[:, i:i + 1] if axis == 1 else score[i:i + 1, :]
        beats = (ci > score) | ((ci == score) & (jb > i))
        rank = rank + jnp.where(beats, 1, 0)
    return jnp.where((rank < N_SEL) & (jb < nsb), 1.0, 0.0)


def _cmp_branch(q4, kc, vc, slope, qpos, tq):
    ncp = kc.shape[0]
    s = _dot_nt(q4, kc)
    cend = lax.broadcasted_iota(jnp.int32, (1, ncp), 1) * STRIDE + (L_CMP - 1)
    s = s - slope * (qpos.astype(F32) - cend.astype(F32))
    pc = _masked_softmax(s, cend <= qpos)
    o_c = _dot(pc.astype(BF16), vc)
    return o_c, pc[0:tq] + pc[tq:2 * tq] + pc[2 * tq:3 * tq] + pc[3 * tq:4 * tq]


def _gate_col(misc, br, g):
    cols = [_lane_pick(misc, GATE_OFF + br * N_NSA + g * HPG + h) for h in range(HPG)]
    return jnp.concatenate(cols, axis=0)


def _slc_matrix(nc, nsb, ncp, width):
    r, c = L_SEL // STRIDE, L_CMP // STRIDE
    offs, counts = np.unique((np.arange(r)[:, None] - np.arange(c)[None, :]).ravel(), return_counts=True)
    m = np.zeros((ncp, width), np.float32)
    for j in range(nsb):
        for off, cnt in zip(offs, counts):
            n = r * j + off
            if 0 <= n < nc:
                m[n, j] += cnt
    return jnp.asarray(m, BF16)


def _alibi_slopes():
    base = 2.0 ** (-8.0 * np.arange(1, N_NSA + 1, dtype=np.float32) / N_NSA)
    return jnp.asarray(base.astype(np.float32).reshape(HPG, N_KV).T)


def _nsa_kernel(slopes_ref, q_ref, misc_ref, cmp_ref, ks_ref, vs_ref, kw_ref, vw_ref, m_ref, e_ref, o_ref,
                ksb, vsb, kwb, vwb, *, tq, tk, nsb):
    g = pl.program_id(1)
    qi = pl.program_id(2)
    t = ks_ref.shape[1]
    nk = t // tk
    wch = LANES
    nwc = t // wch

    @pl.when(qi == 0)
    def _():
        for j in range(nk):
            ksb[j] = ks_ref[:, j * tk:(j + 1) * tk].astype(BF16)
            vsb[j] = _aug_values(vs_ref[:, j * tk:(j + 1) * tk])
        for j in range(nwc):
            kwb[j] = kw_ref[:, j * wch:(j + 1) * wch].astype(BF16)
            vwb[j] = _aug_values(vw_ref[:, j * wch:(j + 1) * wch])

    a = qi * tq
    rows = HPG * tq
    row = lax.broadcasted_iota(jnp.int32, (rows, 1), 0)
    qpos = a + row % tq
    hrow = row // tq
    qpos1 = a + lax.broadcasted_iota(jnp.int32, (tq, 1), 0)
    misc = misc_ref[...]
    slope = _slope_col(slopes_ref, g, hrow)
    q4 = jnp.concatenate([q_ref[:, h * HEAD_DIM:(h + 1) * HEAD_DIM] for h in range(HPG)], axis=0)

    o_c, pg = _cmp_branch(q4, cmp_ref[g], cmp_ref[N_KV + g], slope, qpos, tq)
    hi, mid, lo = _split3(pg)
    mt = m_ref[...]
    p_slct = _dot_nt(mt, hi) + _dot_nt(mt, mid) + _dot_nt(mt, lo)
    cur = (a + lax.broadcasted_iota(jnp.int32, (1, tq), 1)) // L_SEL
    selt = _select_blocks(p_slct, cur, nsb, 0)
    sel = jnp.concatenate([selt, jnp.zeros((LANES - selt.shape[0], tq), F32)], axis=0).T.astype(BF16)

    def step(kj, carry, diagonal):
        kpos = kj * tk + lax.broadcasted_iota(jnp.int32, (1, tk), 1)
        keep = _dot(sel, e_ref[kj]) > 0.5
        if diagonal:
            keep = keep & (kpos <= qpos1)
        s4 = _dot(q4, ksb[kj])
        drel = (kpos - a).astype(F32)
        out = []
        for h in range(HPG):
            s = s4[h * tq:(h + 1) * tq] + slopes_ref[g, h] * drel
            out.append(_flash_update(jnp.where(keep, s, NEG), *carry[h], vsb[kj]))
        return tuple(out)

    init = tuple((jnp.full((tq, 1), NEG, F32), jnp.zeros((tq, V_AUG), F32)) for _ in range(HPG))
    kd = a // tk
    carry = lax.fori_loop(0, kd, functools.partial(step, diagonal=False), init)
    carry = step(kd, carry, True)

    tqc = tq // wch
    nwin = min(WINDOW // wch + tqc, nwc)
    c0 = qi * tqc + tqc - nwin
    rc = (lax.broadcasted_iota(jnp.int32, (tq, wch), 0) - lax.broadcasted_iota(jnp.int32, (tq, wch), 1))
    s_h = [[] for _ in range(HPG)]
    cis = []
    for i in range(nwin):
        exists = c0 + i >= 0
        ci = jnp.maximum(c0 + i, 0)
        cis.append(ci)
        s4 = _dot(q4, kwb[ci])
        rel = (tqc - nwin + i) * wch
        drel = (rel + lax.broadcasted_iota(jnp.int32, (1, wch), 1)).astype(F32)
        keep = exists
        if -(wch - 1) - rel < 0:
            keep = keep & (rc >= rel)
        if (tq - 1) - rel >= WINDOW:
            keep = keep & (rc < rel + WINDOW)
        for h in range(HPG):
            s_h[h].append(jnp.where(keep, s4[h * tq:(h + 1) * tq] + slopes_ref[g, h] * drel, NEG))
    gate = lambda br, h: _lane_pick(misc, GATE_OFF + br * N_NSA + g * HPG + h)
    outs = []
    for h in range(HPG):
        m = jnp.max(functools.reduce(jnp.maximum, s_h[h]), axis=1, keepdims=True)
        acc = _dot_nt(jnp.exp(s_h[h][0] - m).astype(BF16), vwb[cis[0]])
        for i in range(1, nwin):
            acc = acc + _dot_nt(jnp.exp(s_h[h][i] - m).astype(BF16), vwb[cis[i]])
        outs.append(o_c[h * tq:(h + 1) * tq] * gate(0, h) + _flash_out(carry[h][1], gate(1, h))
                    + _flash_out(acc, gate(2, h)))
    o_ref[...] = jnp.concatenate(outs, axis=1).astype(BF16)


def _nsa_prompt(qn, misc, cmp, nkvt, wint, tq=512, tk=512):
    b, t, _ = qn.shape
    tq = min(tq, t)
    tk = min(tk, t)
    assert tk % tq == 0 and tq % LANES == 0
    ncp = t // STRIDE
    nc = (t - L_CMP) // STRIDE + 1
    nsb = -(-t // L_SEL)
    assert nsb <= LANES
    nsbp = -(-nsb // SUBLANES) * SUBLANES
    m = _slc_matrix(nc, nsb, ncp, nsbp).T
    e = (np.arange(t)[None, :] // L_SEL == np.arange(LANES)[:, None]).astype(np.float32)
    e = jnp.asarray(e.reshape(LANES, t // tk, tk).transpose(1, 0, 2), BF16)
    hw = HPG * HEAD_DIM
    grid_spec = pl.GridSpec(
        grid=(b, N_KV, t // tq),
        in_specs=[pl.BlockSpec(memory_space=pltpu.SMEM),
                  pl.BlockSpec((None, tq, hw), lambda bi, g, qi: (bi, qi, g)),
                  pl.BlockSpec((None, tq, MISC_W), lambda bi, g, qi: (bi, qi, 0)),
                  pl.BlockSpec((None, 2 * N_KV, ncp, HEAD_DIM), lambda bi, g, qi: (bi, 0, 0, 0)),
                  pl.BlockSpec((None, HEAD_DIM, t), lambda bi, g, qi: (bi, 2 * N_KV + g, 0)),
                  pl.BlockSpec((None, HEAD_DIM, t), lambda bi, g, qi: (bi, 3 * N_KV + g, 0)),
                  pl.BlockSpec((None, HEAD_DIM, t), lambda bi, g, qi: (bi, g, 0)),
                  pl.BlockSpec((None, HEAD_DIM, t), lambda bi, g, qi: (bi, N_KV + g, 0)),
                  _const_spec((nsbp, ncp)), _const_spec((t // tk, LANES, tk))],
        out_specs=pl.BlockSpec((None, tq, hw), lambda bi, g, qi: (bi, qi, g)),
        scratch_shapes=[pltpu.VMEM((t // tk, HEAD_DIM, tk), BF16), pltpu.VMEM((t // tk, V_AUG, tk), BF16),
                        pltpu.VMEM((t // LANES, HEAD_DIM, LANES), BF16),
                        pltpu.VMEM((t // LANES, V_AUG, LANES), BF16)])
    return pl.pallas_call(
        functools.partial(_nsa_kernel, tq=tq, tk=tk, nsb=nsb),
        out_shape=jax.ShapeDtypeStruct((b, t, D_NSA), BF16), grid_spec=grid_spec,
        compiler_params=_cparams(("parallel", "parallel", "arbitrary")), name="nsa_prompt",
    )(_alibi_slopes(), qn, misc, cmp, nkvt, nkvt, wint, wint, m, e)


def _fox_dec_kernel(*refs, n_new, ppb):
    pt_ref, q_ref = refs[:2]
    kv_refs = refs[2:2 + ppb]
    lf_refs = refs[2 + ppb:2 + 2 * ppb]
    kvn_ref, lfn_ref, gt_ref, le_ref, o_ref, qbd_sc, m_sc, l_sc, r_sc, acc_sc = refs[2 + 2 * ppb:]
    p = pl.program_id(1)
    page = kvn_ref.shape[2]
    rows = n_new * N_FOX

    def tile_heads(x):
        return jnp.concatenate([x] * n_new, axis=0)

    def accumulate(kt, vt, bias, mask):
        s = _dot(qbd_sc[...], kt) + bias
        if mask is not None:
            s = jnp.where(mask, s, NEG)
        m, alpha, l, pr = _online_softmax_step(s, m_sc[...], l_sc[...])
        m_sc[...] = m
        l_sc[...] = l
        acc_sc[...] = alpha * acc_sc[...] + _dot_nt(pr.astype(BF16), vt)

    @pl.when(p == 0)
    def _():
        q = q_ref[...]
        rep = jnp.concatenate([jnp.broadcast_to(q[t:t + 1, :], (N_FOX, D_FOX)) for t in range(n_new)], axis=0)
        rr = lax.broadcasted_iota(jnp.int32, (rows, D_FOX), 0)
        ll = lax.broadcasted_iota(jnp.int32, (rows, D_FOX), 1)
        qbd_sc[...] = jnp.where(ll // HEAD_DIM == rr % N_FOX, rep, 0.0).astype(BF16)
        m_sc[...] = jnp.full(m_sc.shape, NEG, F32)
        l_sc[...] = jnp.zeros(l_sc.shape, F32)
        r_sc[...] = jnp.zeros(r_sc.shape, F32)
        acc_sc[...] = jnp.zeros(acc_sc.shape, F32)
        cn = _dot3(lfn_ref[...], le_ref[...])
        tkey = lax.broadcasted_iota(jnp.int32, (rows, page), 1)
        tqry = lax.broadcasted_iota(jnp.int32, (rows, page), 0) // N_FOX
        accumulate(kvn_ref[0].astype(BF16), kvn_ref[1].astype(BF16), -tile_heads(cn),
                   (tkey <= tqry) & (tkey < n_new))

    later = r_sc[...]
    suffix = []
    for lf_ref in lf_refs:
        lf = lf_ref[...]
        suffix.append(_dot3(lf, gt_ref[...]) + later)
        later = later + jnp.sum(lf, axis=1, keepdims=True)
    r_sc[...] = later
    accumulate(jnp.concatenate([r[0].astype(BF16) for r in kv_refs], axis=1),
               jnp.concatenate([r[1].astype(BF16) for r in kv_refs], axis=1),
               tile_heads(jnp.concatenate(suffix, axis=1)), None)

    @pl.when(p == pl.num_programs(1) - 1)
    def _():
        o = acc_sc[...] / jnp.maximum(l_sc[...], 1e-30)
        rr = lax.broadcasted_iota(jnp.int32, o.shape, 0)
        ll = lax.broadcasted_iota(jnp.int32, o.shape, 1)
        o = jnp.where(ll // HEAD_DIM == rr % N_FOX, o, 0.0)
        orow = lax.broadcasted_iota(jnp.int32, (SQ, D_FOX), 0)
        out = jnp.zeros((SQ, D_FOX), F32)
        for t in range(n_new):
            out = jnp.where(orow == t, jnp.sum(o[t * N_FOX:(t + 1) * N_FOX], axis=0, keepdims=True), out)
        o_ref[...] = out


def _fox_decode(pt_flat, qf_pad, cache_kvt, cache_lft, kvt_new, lft_new, n_pages, n_new):
    db = qf_pad.shape[0]
    page = cache_kvt.shape[3]
    gt = np.tril(np.ones((page, page), np.float32), -1)
    le = np.triu(np.ones((page, page), np.float32))
    ppb = next(c for c in (8, 4, 2, 1) if n_pages % c == 0)
    pidx = lambda i: (lambda b, p, pt: (pt[b * n_pages + n_pages - 1 - (p * ppb + i)], 0, 0, 0))
    pidx3 = lambda i: (lambda b, p, pt: (pt[b * n_pages + n_pages - 1 - (p * ppb + i)], 0, 0))
    rows = n_new * N_FOX
    grid_spec = pltpu.PrefetchScalarGridSpec(
        num_scalar_prefetch=1, grid=(db, n_pages // ppb),
        in_specs=[pl.BlockSpec((None, SQ, D_FOX), lambda b, p, pt: (b, 0, 0))]
        + [pl.BlockSpec((None, 2, D_FOX, page), pidx(i)) for i in range(ppb)]
        + [pl.BlockSpec((None, N_FOX, page), pidx3(i)) for i in range(ppb)]
        + [pl.BlockSpec((None, 2, D_FOX, page), lambda b, p, pt: (b, 0, 0, 0)),
           pl.BlockSpec((None, N_FOX, page), lambda b, p, pt: (b, 0, 0)),
           _const_spec((page, page)), _const_spec((page, page))],
        out_specs=pl.BlockSpec((None, SQ, D_FOX), lambda b, p, pt: (b, 0, 0)),
        scratch_shapes=[pltpu.VMEM((rows, D_FOX), BF16), pltpu.VMEM((rows, 1), F32), pltpu.VMEM((rows, 1), F32),
                        pltpu.VMEM((N_FOX, 1), F32), pltpu.VMEM((rows, D_FOX), F32)])
    return pl.pallas_call(
        functools.partial(_fox_dec_kernel, n_new=n_new, ppb=ppb),
        out_shape=jax.ShapeDtypeStruct((db, SQ, D_FOX), F32), grid_spec=grid_spec,
        compiler_params=_cparams(("parallel", "arbitrary")), name="fox_decode",
    )(pt_flat, qf_pad, *([cache_kvt] * ppb), *([cache_lft] * ppb), kvt_new, lft_new,
      jnp.asarray(gt, BF16), jnp.asarray(le, BF16))


def _nsa_pages_kernel(*refs, ppb):
    pg_refs = refs[1:1 + ppb]
    new_ref, pe_ref, w1_ref, w2_ref, ksvs_ref, cmp_ref, tr_sc, x_sc = refs[1 + ppb:]
    p = pl.program_id(1)
    npg = pl.num_programs(1) - 1
    page = new_ref.shape[1]
    rpp = page // STRIDE
    nch = D_KV // LANES

    @pl.when(p < npg)
    def _():
        ksvs_ref[...] = jnp.concatenate(
            [jnp.concatenate([r[2], r[3]], axis=0) for r in pg_refs], axis=1).astype(BF16)
        for k, pg_ref in enumerate(pg_refs):
            for s in range(2):
                for c in range(nch):
                    tr_sc[(k * 2 + s) * nch + c] = pg_ref[s, c * LANES:(c + 1) * LANES, :].T
            r0 = pl.multiple_of((p * ppb + k) * rpp, rpp)
            for l in range(STRIDE):
                for c in range(2 * nch):
                    rows = tr_sc[k * 2 * nch + c, pl.ds(l, rpp, stride=STRIDE), :]
                    for i in range(2):
                        x_sc[2 * c + i, pl.ds(r0, rpp), l * HEAD_DIM:(l + 1) * HEAD_DIM] = (
                            rows[:, i * HEAD_DIM:(i + 1) * HEAD_DIM])

    @pl.when(p == npg)
    def _():
        pad = [jnp.zeros((2 * D_KV, (ppb - 1) * page), F32)] if ppb > 1 else []
        ksvs_ref[...] = jnp.concatenate([new_ref[...]] + pad, axis=1).astype(BF16)
        for j in range(2 * N_KV):
            s = j // N_KV
            cmp_ref[j] = _compress_rows(x_sc[j], pe_ref.at[s], w1_ref.at[s], w2_ref.at[s]).astype(BF16)


def _nsa_pages(pt_flat, cache_nsat, ksvst_new, pe2, w1, w2, n_pages):
    db = ksvst_new.shape[0]
    page = cache_nsat.shape[3]
    past = n_pages * page
    rows = past // STRIDE
    ppb = next(c for c in (16, 8, 4, 2, 1) if n_pages % c == 0)
    nsteps = n_pages // ppb
    pidx = lambda i: (lambda b, p, pt: (pt[b * n_pages + jnp.minimum(p * ppb + i, n_pages - 1)], 0, 0, 0))
    grid_spec = pltpu.PrefetchScalarGridSpec(
        num_scalar_prefetch=1, grid=(db, nsteps + 1),
        in_specs=[pl.BlockSpec((None, 4, D_KV, page), pidx(i)) for i in range(ppb)]
        + [pl.BlockSpec((None, 2 * D_KV, page), lambda b, p, pt: (b, 0, 0)),
           _const_spec((2, 2, CMP_ROW)), _const_spec((2, L_CMP * HEAD_DIM, LANES)),
           _const_spec((2, LANES, HEAD_DIM))],
        out_specs=(pl.BlockSpec((None, 2 * D_KV, ppb * page), lambda b, p, pt: (b, 0, p)),
                   pl.BlockSpec((None, 2 * N_KV, rows, HEAD_DIM), lambda b, p, pt: (b, 0, 0, 0))),
        scratch_shapes=[pltpu.VMEM((ppb * 2 * D_KV // LANES, page, LANES), F32),
                        pltpu.VMEM((2 * N_KV, rows, CMP_ROW), F32)])
    return pl.pallas_call(
        functools.partial(_nsa_pages_kernel, ppb=ppb),
        out_shape=(jax.ShapeDtypeStruct((db, 2 * D_KV, past + ppb * page), BF16),
                   jax.ShapeDtypeStruct((db, 2 * N_KV, rows, HEAD_DIM), BF16)),
        grid_spec=grid_spec, compiler_params=_cparams(("parallel", "arbitrary")), name="nsa_pages",
    )(pt_flat, *([cache_nsat] * ppb), ksvst_new, pe2, w1, w2)


def _nsa_dec_kernel(slopes_ref, q_ref, misc_ref, cmp_ref, ksvs_ref, win_ref, winn_ref, m_ref, e_ref, o_ref,
                    selx_sc, *, past, nsb):
    tq = SQ
    rows = HPG * tq
    row = lax.broadcasted_iota(jnp.int32, (rows, 1), 0)
    qpos = past + row % tq
    qposf = qpos.astype(F32)
    hrow = row // tq
    misc = misc_ref[...]
    q = q_ref[...]
    cur = (past + lax.broadcasted_iota(jnp.int32, (tq, 1), 0)) // L_SEL
    nkeys = ksvs_ref.shape[1]

    o_cs, sels, q4s, slopes = [], [], [], []
    for g in range(N_KV):
        slope = _slope_col(slopes_ref, g, hrow)
        q4 = jnp.concatenate(
            [q[:, (g * HPG + h) * HEAD_DIM:(g * HPG + h + 1) * HEAD_DIM] for h in range(HPG)], axis=0).astype(BF16)
        o_c, pg = _cmp_branch(q4, cmp_ref[g], cmp_ref[N_KV + g], slope, qpos, tq)
        sels.append(_select_blocks(_dot3(pg, m_ref[...]), cur, nsb, 1))
        o_cs.append(o_c)
        q4s.append(q4)
        slopes.append(slope)
    selx_sc[...] = _dot(jnp.concatenate(sels, axis=0).astype(BF16), e_ref[...])

    kpos = lax.broadcasted_iota(jnp.int32, (1, nkeys), 1)
    nwin = win_ref.shape[2]
    wpos = (past - nwin) + lax.broadcasted_iota(jnp.int32, (1, nwin + winn_ref.shape[2]), 1)
    outs = []
    for g in range(N_KV):
        q4, slope = q4s[g], slopes[g]
        gr = slice(g * HEAD_DIM, (g + 1) * HEAD_DIM)
        s = _dot(q4, ksvs_ref[gr, :]) - slope * (qposf - kpos.astype(F32))
        selx = selx_sc[g * tq:(g + 1) * tq, :]
        mask = (jnp.concatenate([selx] * HPG, axis=0) > 0.5) & (kpos <= qpos)
        o_s = _dot_nt(_masked_softmax(s, mask).astype(BF16), ksvs_ref[D_KV + g * HEAD_DIM:D_KV + (g + 1) * HEAD_DIM, :])
        kw = jnp.concatenate([win_ref[0, gr, :], winn_ref[0, gr, :]], axis=1).astype(BF16)
        vw = jnp.concatenate([win_ref[1, gr, :], winn_ref[1, gr, :]], axis=1).astype(BF16)
        dist = qpos - wpos
        s = _dot(q4, kw) - slope * dist.astype(F32)
        o_w = _dot_nt(_masked_softmax(s, (dist >= 0) & (dist < WINDOW)).astype(BF16), vw)
        o = _gate_col(misc, 0, g) * o_cs[g] + _gate_col(misc, 1, g) * o_s + _gate_col(misc, 2, g) * o_w
        outs += [o[h * tq:(h + 1) * tq] for h in range(HPG)]
    o_ref[...] = jnp.concatenate(outs, axis=1)


def _nsa_decode(qn_pad, misc_pad, cmp, ksvst, wint_state, wint_new, past, n_new):
    db = qn_pad.shape[0]
    nkeys = ksvst.shape[2]
    ncp = cmp.shape[2]
    tk_all = past + n_new
    nc = (tk_all - L_CMP) // STRIDE + 1
    nsb = -(-tk_all // L_SEL)
    width = -(-nsb // LANES) * LANES
    m = _slc_matrix(nc, nsb, ncp, width)
    e = jnp.asarray(np.arange(nkeys)[None, :] // L_SEL == np.arange(width)[:, None], BF16)
    nwin = wint_state.shape[3]
    nwn = wint_new.shape[3]
    per_b3 = lambda b: (b, 0, 0)
    per_b4 = lambda b: (b, 0, 0, 0)
    grid_spec = pl.GridSpec(
        grid=(db,),
        in_specs=[pl.BlockSpec(memory_space=pltpu.SMEM),
                  pl.BlockSpec((None, SQ, D_NSA), per_b3), pl.BlockSpec((None, SQ, MISC_W), per_b3),
                  pl.BlockSpec((None, 2 * N_KV, ncp, HEAD_DIM), per_b4),
                  pl.BlockSpec((None, 2 * D_KV, nkeys), per_b3),
                  pl.BlockSpec((None, 2, D_KV, nwin), per_b4), pl.BlockSpec((None, 2, D_KV, nwn), per_b4),
                  _const_spec((ncp, width)), _const_spec((width, nkeys))],
        out_specs=pl.BlockSpec((None, SQ, D_NSA), per_b3),
        scratch_shapes=[pltpu.VMEM((N_KV * SQ, nkeys), F32)])
    return pl.pallas_call(
        functools.partial(_nsa_dec_kernel, past=past, nsb=nsb),
        out_shape=jax.ShapeDtypeStruct((db, SQ, D_NSA), F32), grid_spec=grid_spec,
        compiler_params=_cparams(("parallel",)), name="nsa_decode",
    )(_alibi_slopes(), qn_pad, misc_pad, cmp, ksvst, wint_state, wint_new, m, e)


def _prep_w_in(w_in, b_forget, b_gate):
    sizes = (D_FOX, D_FOX, D_FOX, N_FOX, D_NSA) + (D_KV,) * 6 + (3 * N_NSA,)
    offs = np.cumsum((0,) + sizes)
    part = lambda i, j=None: w_in[:, offs[i]:offs[(i if j is None else j) + 1]]
    pad = MISC_W - N_FOX - 3 * N_NSA
    misc = jnp.concatenate([part(3), part(11), jnp.zeros((w_in.shape[0], pad), w_in.dtype)], axis=1)
    wa = jnp.concatenate([part(0), part(4), part(5, 6), misc], axis=1).astype(BF16)
    wb = jnp.concatenate([part(1, 2), part(5, 10), misc], axis=1).T.astype(BF16)
    b = jnp.concatenate([b_forget, b_gate, jnp.zeros((pad,), F32)]).astype(F32)
    return wa, wb, b.reshape(1, MISC_W), b.reshape(MISC_W, 1)


def _pos_minor(x):
    nd = x.ndim
    return jnp.transpose(x, tuple(range(nd - 4)) + (nd - 3, nd - 2, nd - 1, nd - 4))


def _pos_major(x):
    nd = x.ndim
    return jnp.transpose(x, tuple(range(nd - 4)) + (nd - 1, nd - 4, nd - 3, nd - 2))


def kernel(x_prompt, x_sample, cache_fox_kv, cache_fox_logf, cache_nsa_kv, state_win_kv, page_table, norm_ffn1,
           ffn1_gate, ffn1_up, ffn1_down, norm_mix, w_in, b_forget, b_gate, phi_pe, phi_w1, phi_w2, w_out,
           norm_ffn2, ffn2_gate, ffn2_up, ffn2_down, norm_final):
    depth = norm_ffn1.shape[0]
    bsz, t, _ = x_prompt.shape
    db, s_new, _ = x_sample.shape
    n_pool, page = cache_fox_kv.shape[1], cache_fox_kv.shape[2]
    n_pages = page_table.shape[1]
    past = n_pages * page
    wb_len = state_win_kv.shape[2]
    wp = min(WINDOW, t)
    assert s_new <= SQ and wb_len == min(WINDOW, past)

    xp = x_prompt
    xs = x_sample.reshape(1, db * s_new, D_MODEL)
    fox_kvt_pool = _pos_minor(cache_fox_kv).reshape(depth * n_pool, 2, D_FOX, page)
    fox_lft_pool = jnp.swapaxes(cache_fox_logf, 2, 3).reshape(depth * n_pool, N_FOX, page)
    nsa_kvt_pool = _pos_minor(cache_nsa_kv).reshape(depth * n_pool, 4, D_KV, page)
    win_t = _pos_minor(state_win_kv).reshape(depth, db, 2, D_KV, wb_len)
    gfin = norm_final.reshape(1, D_MODEL)
    outs = [[] for _ in range(8)]

    def pad_rows(a):
        a = a.reshape(db, s_new, -1).astype(F32)
        return jnp.pad(a, ((0, 0), (0, SQ - s_new), (0, 0)))

    def new_cols(a_t, lead):
        a_t = jnp.transpose(a_t.reshape(a_t.shape[1], db, s_new), (1, 0, 2))
        return jnp.pad(a_t, ((0, 0), (0, 0), (0, page - s_new))).reshape((db,) + lead + (page,))

    for l in range(depth):
        last = l == depth - 1
        g1 = norm_ffn1[l].reshape(1, D_MODEL)
        gm = norm_mix[l].reshape(1, D_MODEL)
        g2 = norm_ffn2[l].reshape(1, D_MODEL)
        ffn1 = (ffn1_gate[l].astype(BF16), ffn1_up[l].astype(BF16), ffn1_down[l].astype(BF16))
        ffn2 = (ffn2_gate[l].astype(BF16), ffn2_up[l].astype(BF16), ffn2_down[l].astype(BF16))
        wa, wbt, b_r, b_c = _prep_w_in(w_in[l], b_forget[l], b_gate[l])
        wo = w_out[l].astype(BF16)
        pe2 = phi_pe[l].reshape(2, 2, CMP_ROW)
        w1 = phi_w1[l].astype(BF16)
        w2 = phi_w2[l].astype(BF16)
        pt_flat = (page_table + l * n_pool).reshape(-1).astype(jnp.int32)

        xp = _ffn(xp.reshape(bsz * t, D_MODEL), g1, *ffn1).reshape(bsz, t, D_MODEL)
        xs = _ffn(xs.reshape(db * s_new, D_MODEL), g1, *ffn1).reshape(1, db * s_new, D_MODEL)

        qf, qn, misc, fkvt, nkvt, wint, misct, xcmp = _proj(xp, gm, wa, wbt, b_r, b_c, True)
        o_f = _fox_prompt(qf, fkvt, _cumsum(misct))
        cmp = _compress_prompt(xcmp, pe2, w1, w2, bsz, t)
        o_n = _nsa_prompt(qn, misc, cmp, nkvt, wint)
        xp = _ffn(xp.reshape(bsz * t, D_MODEL), g2, *ffn2,
                  mix=(o_f.reshape(bsz * t, D_FOX), o_n.reshape(bsz * t, D_NSA), wo),
                  gf=gfin if last else None).reshape(bsz, t, D_MODEL)
        outs[0].append(_pos_major(fkvt.reshape(bsz, 2, N_FOX, HEAD_DIM, t)))
        outs[1].append(jnp.swapaxes(misct[:, :N_FOX, :], 1, 2))
        outs[2].append(_pos_major(nkvt.reshape(bsz, 4, N_KV, HEAD_DIM, t)))
        outs[3].append(_pos_major(wint[:, :, t - wp:].reshape(bsz, 2, N_KV, HEAD_DIM, wp)))

        qf, qn, misc, fkvt, nkvt, wint, misct = _proj(xs, gm, wa, wbt, b_r, b_c, False)
        o_f = _fox_decode(pt_flat, pad_rows(qf), fox_kvt_pool, fox_lft_pool, new_cols(fkvt, (2, D_FOX)),
                          new_cols(misct[:, :N_FOX, :], (N_FOX,)), n_pages, s_new)
        ksvst, cmp = _nsa_pages(pt_flat, nsa_kvt_pool, new_cols(nkvt[:, 2 * D_KV:, :], (2 * D_KV,)),
                                pe2, w1, w2, n_pages)
        o_n = _nsa_decode(pad_rows(qn), pad_rows(misc), cmp, ksvst, win_t[l], new_cols(wint, (2, D_KV)),
                          past, s_new)
        o_f = o_f[:, :s_new].reshape(db * s_new, D_FOX).astype(BF16)
        o_n = o_n[:, :s_new].reshape(db * s_new, D_NSA).astype(BF16)
        xs = _ffn(xs.reshape(db * s_new, D_MODEL), g2, *ffn2, mix=(o_f, o_n, wo),
                  gf=gfin if last else None).reshape(1, db * s_new, D_MODEL)
        tok_major = lambda a_t, lead: jnp.transpose(a_t[0], (1, 0)).reshape((db, s_new) + lead)
        win_new = tok_major(wint, (2, N_KV, HEAD_DIM))
        outs[4].append(tok_major(fkvt, (2, N_FOX, HEAD_DIM)))
        outs[5].append(tok_major(misct[:, :N_FOX, :], (N_FOX,)))
        outs[6].append(tok_major(nkvt, (4, N_KV, HEAD_DIM)))
        outs[7].append(jnp.concatenate([state_win_kv[l], win_new], axis=1)[:, -wb_len:])

    y_prompt = xp
    y_sample = xs.reshape(db, s_new, D_MODEL)
    return (y_prompt, y_sample) + tuple(jnp.stack(o) for o in outs)
```
